```python
import math
import jax, jax.numpy as jnp
from jax import lax
import numpy as np

D_MODEL = 1024
BATCH = 16
SEQ = 2048
DEPTH = 2

HEAD_DIM = 64
N_HEADS_FOX = 8
N_HEADS_SB = 8
DIL_PATTERNS = ((128, 1), (512, 4), (2048, 16))
N_DIL_GROUPS = len(DIL_PATTERNS)
N_HEADS_DIL = 4
D_FF = 4 * D_MODEL
ROPE_THETA = 10000.0
Q_BLOCK = 128
EPS = 1e-6
N_BRANCHES = 3

W_FOX = N_HEADS_FOX * HEAD_DIM
W_SB = N_HEADS_SB * HEAD_DIM
W_DIL = N_HEADS_DIL * HEAD_DIM
SZ_FOX_QKV = 3 * W_FOX
SZ_FORGET = N_HEADS_FOX
SZ_SB_QKV = 3 * W_SB
SZ_DIL_QKV = 3 * N_DIL_GROUPS * W_DIL
SZ_GATES = N_BRANCHES * D_MODEL
D_IN = SZ_FOX_QKV + SZ_FORGET + SZ_SB_QKV + SZ_DIL_QKV + SZ_GATES

kernel_name = "gated_parallel_fox_stickbreak_dilated"


def rms_norm(x, g):
    xf = x.astype(jnp.float32)
    y = xf * lax.rsqrt(jnp.mean(xf * xf, axis=-1, keepdims=True) + EPS)
    return (y * g.astype(jnp.float32)).astype(x.dtype)


def rope(x, positions):
    half = x.shape[-1] // 2
    inv = 1.0 / (ROPE_THETA ** (jnp.arange(half, dtype=jnp.float32) / half))
    ang = positions.astype(jnp.float32)[..., None] * inv
    cos = jnp.cos(ang)[:, :, None, :]
    sin = jnp.sin(ang)[:, :, None, :]
    xf = x.astype(jnp.float32)
    x1, x2 = xf[..., :half], xf[..., half:]
    return jnp.concatenate([x1 * cos - x2 * sin, x2 * cos + x1 * sin], axis=-1).astype(x.dtype)


def forgetting_attention(q, k, v, f_cum):
    B, T, H, D = q.shape
    scale = 1.0 / math.sqrt(D)
    f_t = jnp.transpose(f_cum, (0, 2, 1))
    outs = []
    for i in range(T // Q_BLOCK):
        lo, hi = i * Q_BLOCK, (i + 1) * Q_BLOCK
        s = jnp.einsum('bqhd,bkhd->bhqk', q[:, lo:hi], k[:, :hi]).astype(jnp.float32) * scale
        s = s + f_t[:, :, lo:hi, None] - f_t[:, :, None, :hi]
        mask = (lo + np.arange(Q_BLOCK))[:, None] >= np.arange(hi)[None, :]
        s = jnp.where(mask, s, -jnp.inf)
        p = jax.nn.softmax(s, axis=-1).astype(v.dtype)
        outs.append(jnp.einsum('bhqk,bkhd->bqhd', p, v[:, :hi]))
    return jnp.concatenate(outs, axis=1)


def stick_breaking_attention(q, k, v):
    B, T, H, D = q.shape
    scale = 1.0 / math.sqrt(D)
    outs = []
    for i in range(T // Q_BLOCK):
        lo, hi = i * Q_BLOCK, (i + 1) * Q_BLOCK
        z = jnp.einsum('bqhd,bkhd->bhqk', q[:, lo:hi], k[:, :hi]).astype(jnp.float32) * scale
        mask = (lo + np.arange(Q_BLOCK))[:, None] > np.arange(hi)[None, :]
        log_not = jnp.where(mask, jax.nn.log_sigmoid(-z), 0.0)
        later = lax.cumsum(log_not, axis=3, reverse=True) - log_not
        a = jnp.where(mask, jnp.exp(jax.nn.log_sigmoid(z) + later), 0.0)
        outs.append(jnp.einsum('bhqk,bkhd->bqhd', a.astype(v.dtype), v[:, :hi]))
    return jnp.concatenate(outs, axis=1)


def dilated_window_attention(q, k, v, window, dilation):
    B, T, H, D = q.shape
    n = T // dilation
    W = window // dilation
    Z = B * dilation
    scale = 1.0 / math.sqrt(D)

    def to_streams(a):
        return a.reshape(B, n, dilation, H, D).transpose(0, 2, 1, 3, 4).reshape(Z, n, H, D)

    qs, ks, vs = to_streams(q), to_streams(k), to_streams(v)
    qb = math.gcd(n, Q_BLOCK)
    nb = n // qb
    pad = ((0, 0), (W, 0), (0, 0), (0, 0))
    kp, vp = jnp.pad(ks, pad), jnp.pad(vs, pad)
    idx = np.arange(nb)[:, None] * qb + np.arange(qb + W)[None, :]
    kblk, vblk = kp[:, idx], vp[:, idx]
    qblk = qs.reshape(Z, nb, qb, H, D)
    s = jnp.einsum('znqhd,znkhd->znhqk', qblk, kblk).astype(jnp.float32) * scale
    dist = np.arange(qb)[:, None] + W - np.arange(qb + W)[None, :]
    band = (dist >= 0) & (dist <= W)
    mask = band[None] & (idx - W >= 0)[:, None, :]
    s = jnp.where(mask[None, :, None], s, -jnp.inf)
    m = jnp.max(s, axis=-1, keepdims=True)
    p = jnp.exp(s - m)
    den = jnp.sum(p, axis=-1, keepdims=True)
    o = jnp.einsum('znhqk,znkhd->znqhd', (p / den).astype(v.dtype), vblk)
    lse = jnp.transpose((m + jnp.log(den))[..., 0], (0, 1, 3, 2))
    o = o.reshape(B, dilation, n, H, D).transpose(0, 2, 1, 3, 4).reshape(B, T, H, D)
    lse = lse.reshape(B, dilation, n, H).transpose(0, 2, 1, 3).reshape(B, T, H)
    return o, lse


def setup_inputs(seed: int = 0) -> dict:
    key = jax.random.key(seed)
    ks = jax.random.split(key, 17)
    f32 = jnp.float32

    def nrm(k, shape, fan_in, mult=1.0):
        return jax.random.normal(k, shape, f32) * (mult * fan_in ** -0.5)

    def gain(k, shape):
        return 1.0 + 0.05 * jax.random.normal(k, shape, f32)

    x = jax.random.normal(ks[0], (BATCH, SEQ, D_MODEL), f32)
    positions = jnp.broadcast_to(jnp.arange(SEQ, dtype=jnp.int32), (BATCH, SEQ))
    return {
        "x": x,
        "positions": positions,
        "attn_norm": gain(ks[1], (DEPTH, D_MODEL)),
        "w_in": nrm(ks[2], (DEPTH, D_MODEL, D_IN), D_MODEL),
        "b_forget": 2.0 + 0.1 * jax.random.normal(ks[3], (DEPTH, N_HEADS_FOX), f32),
        "q_norm_fox": gain(ks[4], (DEPTH, HEAD_DIM)),
        "k_norm_fox": gain(ks[5], (DEPTH, HEAD_DIM)),
        "q_norm_dil": gain(ks[6], (DEPTH, HEAD_DIM)),
        "k_norm_dil": gain(ks[7], (DEPTH, HEAD_DIM)),
        "w_up_fox": nrm(ks[8], (DEPTH, W_FOX, D_MODEL), W_FOX),
        "w_up_sb": nrm(ks[9], (DEPTH, W_SB, D_MODEL), W_SB),
        "w_up_dil": nrm(ks[10], (DEPTH, W_DIL, D_MODEL), W_DIL),
        "w_out": nrm(ks[11], (DEPTH, D_MODEL, D_MODEL), D_MODEL),
        "mlp_norm": gain(ks[12], (DEPTH, D_MODEL)),
        "w_mlp_in": nrm(ks[13], (DEPTH, D_MODEL, D_FF), D_MODEL),
        "w_mlp_out": nrm(ks[14], (DEPTH, D_FF, D_MODEL), D_FF, 0.5),
    }


def reference(x, positions, attn_norm, w_in, b_forget, q_norm_fox, k_norm_fox, q_norm_dil, k_norm_dil,
              w_up_fox, w_up_sb, w_up_dil, w_out, mlp_norm, w_mlp_in, w_mlp_out):
    B, T, _ = x.shape
    o1 = SZ_FOX_QKV
    o2 = o1 + SZ_FORGET
    o3 = o2 + SZ_SB_QKV
    o4 = o3 + SZ_DIL_QKV
    for l in range(DEPTH):
        h = rms_norm(x, attn_norm[l])
        proj = h @ w_in[l]

        fox = proj[..., :o1].reshape(B, T, 3, N_HEADS_FOX, HEAD_DIM)
        qa = rms_norm(fox[:, :, 0], q_norm_fox[l])
        ka = rms_norm(fox[:, :, 1], k_norm_fox[l])
        va = fox[:, :, 2]
        log_f = jax.nn.log_sigmoid(proj[..., o1:o2].astype(jnp.float32) + b_forget[l].astype(jnp.float32))
        f_cum = jnp.cumsum(log_f, axis=1)
        out_a = forgetting_attention(qa, ka, va, f_cum)

        sb = proj[..., o2:o3].reshape(B, T, 3, N_HEADS_SB, HEAD_DIM)
        out_b = stick_breaking_attention(sb[:, :, 0], sb[:, :, 1], sb[:, :, 2])

        dil = proj[..., o3:o4].reshape(B, T, 3, N_DIL_GROUPS * N_HEADS_DIL, HEAD_DIM)
        qc = rope(rms_norm(dil[:, :, 0], q_norm_dil[l]), positions)
        kc = rope(rms_norm(dil[:, :, 1], k_norm_dil[l]), positions)
        vc = dil[:, :, 2]
        group_o, group_lse = [], []
        for g, (window, dilation) in enumerate(DIL_PATTERNS):
            sl = slice(g * N_HEADS_DIL, (g + 1) * N_HEADS_DIL)
            o_g, lse_g = dilated_window_attention(qc[:, :, sl], kc[:, :, sl], vc[:, :, sl], window, dilation)
            group_o.append(o_g)
            group_lse.append(lse_g)
        wts = jax.nn.softmax(jnp.stack(group_lse, axis=0), axis=0)
        out_c = jnp.sum(wts[..., None].astype(vc.dtype) * jnp.stack(group_o, axis=0), axis=0)

        gates = jax.nn.sigmoid(proj[..., o4:].astype(jnp.float32)).astype(x.dtype).reshape(B, T, N_BRANCHES, D_MODEL)
        y_a = out_a.reshape(B, T, W_FOX) @ w_up_fox[l]
        y_b = out_b.reshape(B, T, W_SB) @ w_up_sb[l]
        y_c = out_c.reshape(B, T, W_DIL) @ w_up_dil[l]
        merged = gates[:, :, 0] * y_a + gates[:, :, 1] * y_b + gates[:, :, 2] * y_c
        x = x + merged @ w_out[l]

        h2 = rms_norm(x, mlp_norm[l])
        x = x + jnp.square(jax.nn.relu(h2 @ w_mlp_in[l])) @ w_mlp_out[l]
    return x
```

```python
import functools
import math

import numpy as np
import jax
import jax.numpy as jnp
from jax import lax
from jax.experimental import pallas as pl
from jax.experimental.pallas import tpu as pltpu

D_MODEL = 1024
HEAD_DIM = 64
N_HEADS_FOX = 8
N_HEADS_SB = 8
DIL_PATTERNS = ((128, 1), (512, 4), (2048, 16))
N_DIL_GROUPS = len(DIL_PATTERNS)
N_HEADS_DIL = 4
D_FF = 4 * D_MODEL
ROPE_THETA = 10000.0
EPS = 1e-6
N_BRANCHES = 3

W_FOX = N_HEADS_FOX * HEAD_DIM
W_SB = N_HEADS_SB * HEAD_DIM
W_DIL = N_HEADS_DIL * HEAD_DIM
SZ_FOX_QKV = 3 * W_FOX
SZ_FORGET = N_HEADS_FOX
SZ_SB_QKV = 3 * W_SB
SZ_DIL_QKV = 3 * N_DIL_GROUPS * W_DIL
SZ_GATES = N_BRANCHES * D_MODEL

LANES = 128
PAIR = LANES
SCALE = 1.0 / math.sqrt(HEAD_DIM)
NEG = -1e30

TM = 512
TQ = 256
TK = 256
DQ = 128
PROJ_CHUNK = 768
FF_CHUNK = 1024
VMEM_LIMIT = 56 * 1024 * 1024

F32 = jnp.float32
BF16 = jnp.bfloat16


def _params(sem):
    return pltpu.CompilerParams(dimension_semantics=sem, vmem_limit_bytes=VMEM_LIMIT)


def _resident(shape):
    nd = len(shape)
    return pl.BlockSpec(shape, lambda *_: (0,) * nd, pipeline_mode=pl.Buffered(1))


def _dot(a, b):
    return jnp.dot(a, b, preferred_element_type=F32)


def _dot_nt(a, b):
    return lax.dot_general(a, b, (((1,), (1,)), ((), ())), preferred_element_type=F32)


def _split3(x):
    hi = x.astype(BF16)
    r1 = x - hi.astype(F32)
    mid = r1.astype(BF16)
    lo = (r1 - mid.astype(F32)).astype(BF16)
    return hi, mid, lo


def _split2(x):
    hi = x.astype(BF16)
    lo = (x - hi.astype(F32)).astype(BF16)
    return hi, lo


def _lane_lo():
    return lax.broadcasted_iota(jnp.int32, (1, LANES), 1) < HEAD_DIM


def _inproj_kernel(x_ref, g_ref, w_ref, fox_ref, sb_ref, dil_ref, gate_ref, flog_ref):
    x = x_ref[...]
    ms = jnp.mean(x * x, axis=-1, keepdims=True)
    h = (x * lax.rsqrt(ms + EPS) * g_ref[...]).astype(BF16)
    col = 0
    for out_ref, width in ((fox_ref, SZ_FOX_QKV), (sb_ref, SZ_SB_QKV), (dil_ref, SZ_DIL_QKV), (gate_ref, SZ_GATES)):
        for c in range(0, width, PROJ_CHUNK):
            out_ref[:, c:c + PROJ_CHUNK] = _dot(h, w_ref[:, col + c:col + c + PROJ_CHUNK]).astype(BF16)
        col += width
    flog_ref[...] = _dot(h, w_ref[:, col:col + LANES])


def _inproj(x2, g, w):
    n = x2.shape[0]
    wcols = w.shape[1]
    row = lambda i: (i, 0)
    return pl.pallas_call(
        _inproj_kernel,
        grid=(n // TM,),
        in_specs=[pl.BlockSpec((TM, D_MODEL), row), _resident((1, D_MODEL)), _resident((D_MODEL, wcols))],
        out_specs=[pl.BlockSpec((TM, SZ_FOX_QKV), row), pl.BlockSpec((TM, SZ_SB_QKV), row),
                   pl.BlockSpec((TM, SZ_DIL_QKV), row), pl.BlockSpec((TM, SZ_GATES), row),
                   pl.BlockSpec((TM, LANES), row)],
        out_shape=[jax.ShapeDtypeStruct((n, SZ_FOX_QKV), BF16), jax.ShapeDtypeStruct((n, SZ_SB_QKV), BF16),
                   jax.ShapeDtypeStruct((n, SZ_DIL_QKV), BF16), jax.ShapeDtypeStruct((n, SZ_GATES), BF16),
                   jax.ShapeDtypeStruct((n, LANES), F32)],
        compiler_params=_params(("arbitrary",)),
        name="inproj",
    )(x2, g, w)


def _pair_rms(x, lo):
    x2 = x * x
    s_lo = jnp.sum(jnp.where(lo, x2, 0.0), axis=-1, keepdims=True)
    s_hi = jnp.sum(jnp.where(lo, 0.0, x2), axis=-1, keepdims=True)
    r = jnp.where(lo, lax.rsqrt(s_lo * (1.0 / HEAD_DIM) + EPS), lax.rsqrt(s_hi * (1.0 / HEAD_DIM) + EPS))
    return x * r


def _prep_kernel(fq_in, fk_in, flog_ref, dq_in, dk_in, pos_ref, gains_ref, bf_ref, rope_ref,
                 fq_out, fk_out, frow_out, fc_out, dq_out, dk_out, carry_ref):
    t = pl.program_id(1)
    lo = _lane_lo()
    gq_fox, gk_fox, gq_dil, gk_dil = (gains_ref[i:i + 1, :] for i in range(4))

    for p in range(W_FOX // PAIR):
        sl = slice(p * PAIR, (p + 1) * PAIR)
        fq_out[:, sl] = (_pair_rms(fq_in[:, sl].astype(F32), lo) * gq_fox).astype(BF16)
        fk_out[:, sl] = (_pair_rms(fk_in[:, sl].astype(F32), lo) * gk_fox).astype(BF16)

    ang = pos_ref[...].astype(F32) * rope_ref[0:1, :]
    cos = jnp.cos(ang)
    sin_signed = jnp.sin(ang) * rope_ref[1:2, :]
    first_half = rope_ref[2:3, :] > 0.5

    def rope(x):
        partner = jnp.where(first_half, pltpu.roll(x, LANES - HEAD_DIM // 2, 1), pltpu.roll(x, HEAD_DIM // 2, 1))
        return x * cos + partner * sin_signed

    for p in range(N_DIL_GROUPS * W_DIL // PAIR):
        sl = slice(p * PAIR, (p + 1) * PAIR)
        dq_out[:, sl] = rope(_pair_rms(dq_in[:, sl].astype(F32), lo) * gq_dil).astype(BF16)
        dk_out[:, sl] = rope(_pair_rms(dk_in[:, sl].astype(F32), lo) * gk_dil).astype(BF16)

    @pl.when(t == 0)
    def _():
        carry_ref[...] = jnp.zeros_like(carry_ref)

    z = flog_ref[...] + bf_ref[...]
    log_f = jnp.minimum(z, 0.0) - jnp.log(1.0 + jnp.exp(-jnp.abs(z)))
    r = lax.broadcasted_iota(jnp.int32, (TM, TM), 0)
    c = lax.broadcasted_iota(jnp.int32, (TM, TM), 1)
    tri = jnp.where(r >= c, 1.0, 0.0).astype(BF16)
    hi, mid, lw = _split3(log_f)
    f_cum = (_dot(tri, hi) + _dot(tri, mid) + _dot(tri, lw)) + carry_ref[0:1, :]
    carry_ref[...] = jnp.broadcast_to(f_cum[TM - 1:TM, :], carry_ref.shape)
    frow_out[...] = jnp.transpose(f_cum)[0:N_HEADS_FOX, :]
    er = lax.broadcasted_iota(jnp.int32, (LANES, W_FOX), 0)
    ec = lax.broadcasted_iota(jnp.int32, (LANES, W_FOX), 1)
    expand = jnp.where((ec >= er * HEAD_DIM) & (ec < (er + 1) * HEAD_DIM), 1.0, 0.0).astype(BF16)
    hi, mid, lw = _split3(f_cum)
    fc_out[...] = _dot(hi, expand) + _dot(mid, expand) + _dot(lw, expand)


def _prep(p_fox, flog, p_dil, pos, gains, bf, rope_tab, batch, seq):
    n = p_fox.shape[0]
    nt = seq // TM
    wd = N_DIL_GROUPS * W_DIL
    row = lambda b, t: (b * nt + t, 0)
    col1 = lambda b, t: (b * nt + t, 1)
    return pl.pallas_call(
        _prep_kernel,
        grid=(batch, nt),
        in_specs=[pl.BlockSpec((TM, W_FOX), row), pl.BlockSpec((TM, W_FOX), col1),
                  pl.BlockSpec((TM, LANES), row),
                  pl.BlockSpec((TM, wd), row), pl.BlockSpec((TM, wd), col1),
                  pl.BlockSpec((TM, 1), row),
                  _resident((4, LANES)), _resident((1, LANES)), _resident((3, LANES))],
        out_specs=[pl.BlockSpec((TM, W_FOX), row), pl.BlockSpec((TM, W_FOX), row),
                   pl.BlockSpec((None, N_HEADS_FOX, TM), lambda b, t: (b, 0, t)),
                   pl.BlockSpec((TM, W_FOX), row),
                   pl.BlockSpec((TM, wd), row), pl.BlockSpec((TM, wd), row)],
        out_shape=[jax.ShapeDtypeStruct((n, W_FOX), BF16), jax.ShapeDtypeStruct((n, W_FOX), BF16),
                   jax.ShapeDtypeStruct((batch, N_HEADS_FOX, seq), F32),
                   jax.ShapeDtypeStruct((n, W_FOX), F32),
                   jax.ShapeDtypeStruct((n, wd), BF16), jax.ShapeDtypeStruct((n, wd), BF16)],
        scratch_shapes=[pltpu.VMEM((8, LANES), F32)],
        compiler_params=_params(("arbitrary", "arbitrary")),
        name="prep",
    )(p_fox, p_fox, flog, p_dil, p_dil, pos, gains, bf, rope_tab)


def _fox_kernel(q_ref, k_ref, v_ref, frow_ref, fc_ref, o_ref, *, seq):
    lo = _lane_lo()
    r = lax.broadcasted_iota(jnp.int32, (TQ, TK), 0)
    c = lax.broadcasted_iota(jnp.int32, (TQ, TK), 1)
    causal = c <= r
    zero = jnp.zeros((), BF16)

    for qi in range(seq // TQ):
        rows = slice(qi * TQ, (qi + 1) * TQ)
        q = q_ref[rows, :]
        qh = (jnp.where(lo, q, zero), jnp.where(lo, zero, q))
        fc = fc_ref[rows, :]
        ft = (fc[:, 0:1], fc[:, HEAD_DIM:HEAD_DIM + 1])

        def step(j, carry, diag):
            start = pl.multiple_of(j * TK, TK)
            kc = k_ref[pl.ds(start, TK), :]
            vc = v_ref[pl.ds(start, TK), :]
            vh = (jnp.where(lo, vc, zero), jnp.where(lo, zero, vc))
            acc = carry[4]
            stats, alphas, pv = [], [], None
            for hh in range(2):
                m_prev, l_prev = carry[2 * hh], carry[2 * hh + 1]
                s = _dot_nt(qh[hh], kc) - frow_ref[hh, pl.ds(j, 1), :]
                if diag:
                    s = jnp.where(causal, s, NEG)
                m_new = jnp.maximum(m_prev, jnp.max(s, axis=-1, keepdims=True) + ft[hh])
                alpha = jnp.exp(m_prev - m_new)
                p = jnp.exp(s - (m_new - ft[hh]))
                l_new = alpha * l_prev + jnp.sum(p, axis=-1, keepdims=True)
                d = _dot(p.astype(BF16), vh[hh])
                pv = d if pv is None else pv + d
                stats += [m_new, l_new]
                alphas.append(alpha)
            acc = acc * jnp.where(lo, alphas[0], alphas[1]) + pv
            return (*stats, acc)

        init_m = jnp.full((TQ, 1), NEG, F32)
        init_l = jnp.zeros((TQ, 1), F32)
        carry = (init_m, init_l, init_m, init_l, jnp.zeros((TQ, LANES), F32))
        if qi > 0:
            carry = lax.fori_loop(0, qi, lambda j, cr: step(j, cr, False), carry)
        carry = step(qi, carry, True)
        o_ref[rows, :] = (carry[4] * jnp.where(lo, 1.0 / carry[1], 1.0 / carry[3])).astype(BF16)


def _fox_attention(fq, fk, p_fox, frow, fc, batch, seq):
    n = fq.shape[0]
    npair = W_FOX // PAIR
    frow5 = frow.reshape(batch, npair, 2, seq // TK, TK)
    blk = lambda b, p: (b, p)
    return pl.pallas_call(
        functools.partial(_fox_kernel, seq=seq),
        grid=(batch, npair),
        in_specs=[pl.BlockSpec((seq, PAIR), blk), pl.BlockSpec((seq, PAIR), blk),
                  pl.BlockSpec((seq, PAIR), lambda b, p: (b, 2 * npair + p)),
                  pl.BlockSpec((None, None, 2, seq // TK, TK), lambda b, p: (b, p, 0, 0, 0)),
                  pl.BlockSpec((seq, PAIR), blk)],
        out_specs=pl.BlockSpec((seq, PAIR), blk),
        out_shape=jax.ShapeDtypeStruct((n, W_FOX), BF16),
        compiler_params=_params(("arbitrary", "arbitrary")),
        name="fox_attn",
    )(fq, fk, p_fox, frow5, fc)


def _sb_kernel(q_ref, k_ref, v_ref, o_ref, *, seq):
    lo = _lane_lo()
    r = lax.broadcasted_iota(jnp.int32, (TQ, TK), 0)
    c = lax.broadcasted_iota(jnp.int32, (TQ, TK), 1)
    strict = c < r
    tri = jnp.where(r >= c, 1.0, 0.0).astype(BF16)
    zero = jnp.zeros((), BF16)

    for qi in range(seq // TQ):
        rows = slice(qi * TQ, (qi + 1) * TQ)
        q = q_ref[rows, :] * SCALE
        qh = (jnp.where(lo, q, zero), jnp.where(lo, zero, q))

        def step(j, carry, diag):
            start = pl.multiple_of(j * TK, TK)
            kc = k_ref[pl.ds(start, TK), :]
            vc = v_ref[pl.ds(start, TK), :]
            vh = (jnp.where(lo, vc, zero), jnp.where(lo, zero, vc))
            acc = carry[2]
            sums = []
            for hh in range(2):
                z = _dot_nt(qh[hh], kc)
                log_not = -(jnp.maximum(z, 0.0) + jnp.log(1.0 + jnp.exp(-jnp.abs(z))))
                if diag:
                    log_not = jnp.where(strict, log_not, 0.0)
                hi, lw = _split2(log_not)
                suffix = _dot(hi, tri) + _dot(lw, tri)
                a = jnp.exp(z + suffix + carry[hh])
                if diag:
                    a = jnp.where(strict, a, 0.0)
                acc = acc + _dot(a.astype(BF16), vh[hh])
                sums.append(carry[hh] + suffix[:, 0:1])
            return (*sums, acc)

        init_c = jnp.zeros((TQ, 1), F32)
        carry = step(qi, (init_c, init_c, jnp.zeros((TQ, LANES), F32)), True)
        if qi > 0:
            carry = lax.fori_loop(0, qi, lambda jj, cr: step(qi - 1 - jj, cr, False), carry)
        o_ref[rows, :] = carry[2].astype(BF16)


def _sb_attention(p_sb, batch, seq):
    n = p_sb.shape[0]
    npair = W_SB // PAIR
    return pl.pallas_call(
        functools.partial(_sb_kernel, seq=seq),
        grid=(batch, npair),
        in_specs=[pl.BlockSpec((seq, PAIR), lambda b, p: (b, p)),
                  pl.BlockSpec((seq, PAIR), lambda b, p: (b, npair + p)),
                  pl.BlockSpec((seq, PAIR), lambda b, p: (b, 2 * npair + p))],
        out_specs=pl.BlockSpec((seq, PAIR), lambda b, p: (b, p)),
        out_shape=jax.ShapeDtypeStruct((n, W_SB), BF16),
        compiler_params=_params(("arbitrary", "arbitrary")),
        name="sb_attn",
    )(p_sb, p_sb, p_sb)


def _dil_kernel(q_ref, k_ref, v_ref, o_ref, lse_ref, *, n):
    lo = _lane_lo()
    r = lax.broadcasted_iota(jnp.int32, (DQ, DQ), 0)
    c = lax.broadcasted_iota(jnp.int32, (DQ, DQ), 1)
    mask_cur = c <= r
    mask_prev = c >= r
    zero = jnp.zeros((), BF16)

    def block(start, has_prev):
        rows = pl.ds(start, DQ)
        prev = pl.ds(start - DQ, DQ)
        for pp in range(W_DIL // PAIR):
            cols = slice(pp * PAIR, (pp + 1) * PAIR)
            q = q_ref[rows, cols]
            k_cur, v_cur = k_ref[rows, cols], v_ref[rows, cols]
            if has_prev:
                k_prev, v_prev = k_ref[prev, cols], v_ref[prev, cols]
            outs, lses = [], []
            for hh in range(2):
                keep = lo if hh == 0 else jnp.logical_not(lo)
                qh = jnp.where(keep, q, zero)
                s_cur = jnp.where(mask_cur, _dot_nt(qh, k_cur), NEG)
                m = jnp.max(s_cur, axis=-1, keepdims=True)
                if has_prev:
                    s_prev = jnp.where(mask_prev, _dot_nt(qh, k_prev), NEG)
                    m = jnp.maximum(m, jnp.max(s_prev, axis=-1, keepdims=True))
                p_cur = jnp.exp(s_cur - m)
                den = jnp.sum(p_cur, axis=-1, keepdims=True)
                o = _dot(p_cur.astype(BF16), jnp.where(keep, v_cur, zero))
                if has_prev:
                    p_prev = jnp.exp(s_prev - m)
                    den = den + jnp.sum(p_prev, axis=-1, keepdims=True)
                    o = o + _dot(p_prev.astype(BF16), jnp.where(keep, v_prev, zero))
                outs.append(o * (1.0 / den))
                lses.append(m + jnp.log(den))
            o_ref[rows, cols] = (outs[0] + outs[1]).astype(BF16)
            lse_ref[rows, cols] = jnp.where(lo, lses[0], lses[1])

    block(0, False)
    if n > DQ:
        def body(i, _):
            block(pl.multiple_of(i * DQ, DQ), True)
            return 0
        lax.fori_loop(1, n // DQ, body, 0)


def _dil_attention(dq, dk, p_dil, group, batch, seq):
    window, dil = DIL_PATTERNS[group]
    assert window // dil == DQ
    n = seq // dil
    wd = N_DIL_GROUPS * W_DIL
    nblk_qk = wd // W_DIL
    nblk_v = SZ_DIL_QKV // W_DIL
    voff = 2 * wd // W_DIL
    qv = dq.reshape(batch, n, dil * wd)
    kv = dk.reshape(batch, n, dil * wd)
    vv = p_dil.reshape(batch, n, dil * SZ_DIL_QKV)
    qk_map = lambda b, s: (b, 0, s * nblk_qk + group)
    o, lse = pl.pallas_call(
        functools.partial(_dil_kernel, n=n),
        grid=(batch, dil),
        in_specs=[pl.BlockSpec((None, n, W_DIL), qk_map), pl.BlockSpec((None, n, W_DIL), qk_map),
                  pl.BlockSpec((None, n, W_DIL), lambda b, s: (b, 0, s * nblk_v + voff + group))],
        out_specs=[pl.BlockSpec((None, n, W_DIL), lambda b, s: (b, 0, s)),
                   pl.BlockSpec((None, n, W_DIL), lambda b, s: (b, 0, s))],
        out_shape=[jax.ShapeDtypeStruct((batch, n, dil * W_DIL), BF16),
                   jax.ShapeDtypeStruct((batch, n, dil * W_DIL), F32)],
        compiler_params=_params(("arbitrary", "arbitrary")),
        name=f"dil_attn_g{group}",
    )(qv, kv, vv)
    return o.reshape(batch * seq, W_DIL), lse.reshape(batch * seq, W_DIL)


def _merge_kernel(x_ref, oa_ref, ob_ref, o0_ref, o1_ref, o2_ref, l0_ref, l1_ref, l2_ref, gate_ref,
                  wa_ref, wb_ref, wc_ref, wo_ref, out_ref):
    lses = (l0_ref[...], l1_ref[...], l2_ref[...])
    mx = jnp.maximum(jnp.maximum(lses[0], lses[1]), lses[2])
    es = [jnp.exp(l - mx) for l in lses]
    inv = 1.0 / (es[0] + es[1] + es[2])
    out_c = sum((e * inv) * o[...].astype(F32) for e, o in zip(es, (o0_ref, o1_ref, o2_ref)))
    ys = (_dot(oa_ref[...], wa_ref[...]), _dot(ob_ref[...], wb_ref[...]), _dot(out_c.astype(BF16), wc_ref[...]))
    merged = None
    for b in range(N_BRANCHES):
        gate = 1.0 / (1.0 + jnp.exp(-gate_ref[:, b * D_MODEL:(b + 1) * D_MODEL].astype(F32)))
        merged = gate * ys[b] if merged is None else merged + gate * ys[b]
    out_ref[...] = x_ref[...] + _dot(merged.astype(BF16), wo_ref[...])


def _merge(x2, out_a, out_b, os_, lses, gates, wa, wb, wc, wo):
    n = x2.shape[0]
    row = lambda i: (i, 0)
    tile = lambda w: pl.BlockSpec((TM, w), row)
    return pl.pallas_call(
        _merge_kernel,
        grid=(n // TM,),
        in_specs=[tile(D_MODEL), tile(W_FOX), tile(W_SB)] + [tile(W_DIL)] * 6 + [tile(SZ_GATES)]
                 + [_resident(w.shape) for w in (wa, wb, wc, wo)],
        out_specs=tile(D_MODEL),
        out_shape=jax.ShapeDtypeStruct((n, D_MODEL), F32),
        compiler_params=_params(("arbitrary",)),
        name="merge",
    )(x2, out_a, out_b, *os_, *lses, gates, wa, wb, wc, wo)


def _mlp_kernel(x_ref, g_ref, w1_ref, w2_ref, out_ref):
    x = x_ref[...]
    ms = jnp.mean(x * x, axis=-1, keepdims=True)
    h = (x * lax.rsqrt(ms + EPS) * g_ref[...]).astype(BF16)
    acc = x
    for c in range(0, D_FF, FF_CHUNK):
        u = jnp.maximum(_dot(h, w1_ref[:, c:c + FF_CHUNK]), 0.0)
        acc = acc + _dot((u * u).astype(BF16), w2_ref[c:c + FF_CHUNK, :])
    out_ref[...] = acc


def _mlp(x2, g, w1, w2):
    n = x2.shape[0]
    row = lambda i: (i, 0)
    return pl.pallas_call(
        _mlp_kernel,
        grid=(n // TM,),
        in_specs=[pl.BlockSpec((TM, D_MODEL), row), _resident((1, D_MODEL)),
                  _resident(w1.shape), _resident(w2.shape)],
        out_specs=pl.BlockSpec((TM, D_MODEL), row),
        out_shape=jax.ShapeDtypeStruct((n, D_MODEL), F32),
        compiler_params=_params(("arbitrary",)),
        name="mlp",
    )(x2, g, w1, w2)


def _rope_table():
    half = HEAD_DIM // 2
    lane = np.arange(LANES)
    inv = (1.0 / (ROPE_THETA ** (np.arange(half, dtype=np.float32) / half))).astype(np.float32)
    first = (lane % HEAD_DIM) < half
    return jnp.asarray(np.stack([inv[lane % half], np.where(first, -1.0, 1.0), first.astype(np.float32)]), F32)


def kernel(x, positions, attn_norm, w_in, b_forget, q_norm_fox, k_norm_fox, q_norm_dil, k_norm_dil,
           w_up_fox, w_up_sb, w_up_dil, w_out, mlp_norm, w_mlp_in, w_mlp_out):
    batch, seq, _ = x.shape
    depth = w_in.shape[0]
    n = batch * seq
    o1 = SZ_FOX_QKV
    o2 = o1 + SZ_FORGET
    w_forget = jnp.pad(w_in[:, :, o1:o2], ((0, 0), (0, 0), (0, LANES - SZ_FORGET)))
    w_all = jnp.concatenate([w_in[:, :, :o1], w_in[:, :, o2:], w_forget], axis=2).astype(BF16)
    wa, wb, wc, wo = (w.astype(BF16) for w in (w_up_fox, w_up_sb, w_up_dil, w_out))
    w1, w2 = w_mlp_in.astype(BF16), w_mlp_out.astype(BF16)
    pair = lambda g: jnp.tile(g.astype(F32), (1, LANES // HEAD_DIM))
    gains = jnp.stack([pair(q_norm_fox) * SCALE, pair(k_norm_fox), pair(q_norm_dil) * SCALE, pair(k_norm_dil)], axis=1)
    bf = jnp.pad(b_forget.astype(F32), ((0, 0), (0, LANES - SZ_FORGET)))[:, None, :]
    rope_tab = _rope_table()
    pos = positions.reshape(n, 1)

    x2 = x.reshape(n, D_MODEL)
    for l in range(depth):
        p_fox, p_sb, p_dil, gates, flog = _inproj(x2, attn_norm[l][None, :], w_all[l])
        fq, fk, frow, fc, dq, dk = _prep(p_fox, flog, p_dil, pos, gains[l], bf[l], rope_tab, batch, seq)
        out_a = _fox_attention(fq, fk, p_fox, frow, fc, batch, seq)
        out_b = _sb_attention(p_sb, batch, seq)
        dil = [_dil_attention(dq, dk, p_dil, g, batch, seq) for g in range(N_DIL_GROUPS)]
        x2 = _merge(x2, out_a, out_b, [d[0] for d in dil], [d[1] for d in dil], gates, wa[l], wb[l], wc[l], wo[l])
        x2 = _mlp(x2, mlp_norm[l][None, :], w1[l], w2[l])
    return x2.reshape(batch, seq, D_MODEL)
```

```python
import functools
import math

import numpy as np
import jax
import jax.numpy as jnp
from jax import lax
from jax.experimental import pallas as pl
from jax.experimental.pallas import tpu as pltpu

D_MODEL = 1024
HEAD_DIM = 64
N_HEADS_FOX = 8
N_HEADS_SB = 8
DIL_PATTERNS = ((128, 1), (512, 4), (2048, 16))
N_DIL_GROUPS = len(DIL_PATTERNS)
N_HEADS_DIL = 4
D_FF = 4 * D_MODEL
ROPE_THETA = 10000.0
EPS = 1e-6
N_BRANCHES = 3

W_FOX = N_HEADS_FOX * HEAD_DIM
W_SB = N_HEADS_SB * HEAD_DIM
W_DIL = N_HEADS_DIL * HEAD_DIM
W_DIL_ALL = N_DIL_GROUPS * W_DIL
SZ_FOX_QKV = 3 * W_FOX
SZ_FORGET = N_HEADS_FOX
SZ_SB_QKV = 3 * W_SB
SZ_DIL_QKV = 3 * W_DIL_ALL
SZ_GATES = N_BRANCHES * D_MODEL

LANES = 128
PAIR = LANES
SCALE = 1.0 / math.sqrt(HEAD_DIM)
LOG2E = math.log2(math.e)
NEG = -1e30

TM = 512
TQ = 512
TK = 256
DQ = 128
PROJ_CHUNK = 768
FF_CHUNK = 1024
VMEM_LIMIT = 56 * 1024 * 1024

AUG_F = HEAD_DIM
AUG_ONE = HEAD_DIM + 3

F32 = jnp.float32
BF16 = jnp.bfloat16


def _params(sem):
    return pltpu.CompilerParams(dimension_semantics=sem, vmem_limit_bytes=VMEM_LIMIT)


def _resident(shape):
    nd = len(shape)
    return pl.BlockSpec(shape, lambda *_: (0,) * nd, pipeline_mode=pl.Buffered(1))


def _dot(a, b):
    return jnp.dot(a, b, preferred_element_type=F32)


def _dot_nt(a, b):
    return lax.dot_general(a, b, (((1,), (1,)), ((), ())), preferred_element_type=F32)


def _split3(x):
    hi = x.astype(BF16)
    r1 = x - hi.astype(F32)
    mid = r1.astype(BF16)
    lo = (r1 - mid.astype(F32)).astype(BF16)
    return hi, mid, lo


def _lane_lo():
    return lax.broadcasted_iota(jnp.int32, (1, LANES), 1) < HEAD_DIM


def _inproj_kernel(x_ref, g_ref, w_ref, fox_ref, sb_ref, dil_ref, gate_ref, flog_ref):
    x = x_ref[...]
    ms = jnp.mean(x * x, axis=-1, keepdims=True)
    h = (x * lax.rsqrt(ms + EPS) * g_ref[...]).astype(BF16)
    col = 0
    for out_ref, width in ((fox_ref, SZ_FOX_QKV), (sb_ref, SZ_SB_QKV), (dil_ref, SZ_DIL_QKV), (gate_ref, SZ_GATES)):
        for c in range(0, width, PROJ_CHUNK):
            out_ref[:, c:c + PROJ_CHUNK] = _dot(h, w_ref[:, col + c:col + c + PROJ_CHUNK]).astype(BF16)
        col += width
    flog_ref[...] = _dot(h, w_ref[:, col:col + LANES])


def _inproj(x2, g, w):
    n = x2.shape[0]
    wcols = w.shape[1]
    row = lambda i: (i, 0)
    return pl.pallas_call(
        _inproj_kernel,
        grid=(n // TM,),
        in_specs=[pl.BlockSpec((TM, D_MODEL), row), _resident((1, D_MODEL)), _resident((D_MODEL, wcols))],
        out_specs=[pl.BlockSpec((TM, SZ_FOX_QKV), row), pl.BlockSpec((TM, SZ_SB_QKV), row),
                   pl.BlockSpec((TM, SZ_DIL_QKV), row), pl.BlockSpec((TM, SZ_GATES), row),
                   pl.BlockSpec((TM, LANES), row)],
        out_shape=[jax.ShapeDtypeStruct((n, SZ_FOX_QKV), BF16), jax.ShapeDtypeStruct((n, SZ_SB_QKV), BF16),
                   jax.ShapeDtypeStruct((n, SZ_DIL_QKV), BF16), jax.ShapeDtypeStruct((n, SZ_GATES), BF16),
                   jax.ShapeDtypeStruct((n, LANES), F32)],
        compiler_params=_params(("arbitrary",)),
        name="inproj",
    )(x2, g, w)


def _pair_rms(x, lo):
    x2 = x * x
    s_lo = jnp.sum(jnp.where(lo, x2, 0.0), axis=-1, keepdims=True)
    s_hi = jnp.sum(jnp.where(lo, 0.0, x2), axis=-1, keepdims=True)
    r = jnp.where(lo, lax.rsqrt(s_lo * (1.0 / HEAD_DIM) + EPS), lax.rsqrt(s_hi * (1.0 / HEAD_DIM) + EPS))
    return x * r


def _put_transposed(out_ref, idx, tile, width):
    tt = jnp.transpose(tile).astype(BF16)
    for i in range(TM // width):
        out_ref[idx, i] = tt[:, i * width:(i + 1) * width]


def _prep_kernel(fox_ref, flog_ref, sbq_ref, sbv_ref, dq_in, dk_in, dv_in, pos_ref, gains_ref, bf_ref,
                 rope_ref, eaug_ref, ones_ref,
                 fqt_out, fk_out, fvt_out, sqt_out, svt_out,
                 dq0, dq1, dq2, dk0, dk1, dk2, dv0, dv1, dv2, carry_ref, stage_ref):
    t = pl.program_id(1)
    lo = _lane_lo()
    gq_fox, gk_fox, gq_dil, gk_dil = (gains_ref[i:i + 1, :] for i in range(4))

    @pl.when(t == 0)
    def _():
        carry_ref[...] = jnp.zeros_like(carry_ref)

    z = flog_ref[...] + bf_ref[...]
    log_f = jnp.minimum(z, 0.0) - jnp.log(1.0 + jnp.exp(-jnp.abs(z)))
    r = lax.broadcasted_iota(jnp.int32, (TM, TM), 0)
    c = lax.broadcasted_iota(jnp.int32, (TM, TM), 1)
    tri = jnp.where(r >= c, 1.0, 0.0).astype(BF16)
    hi, mid, lw = _split3(log_f)
    f_cum = (_dot(tri, hi) + _dot(tri, mid) + _dot(tri, lw)) + carry_ref[0:1, :]
    carry_ref[...] = jnp.broadcast_to(f_cum[TM - 1:TM, :], carry_ref.shape)
    hi, mid, lw = _split3(f_cum * LOG2E)
    aug = _dot(hi, eaug_ref[0]) + _dot(mid, eaug_ref[1]) + _dot(lw, eaug_ref[2]) + ones_ref[...]

    for p in range(W_FOX // PAIR):
        sl = slice(p * PAIR, (p + 1) * PAIR)
        qn = _pair_rms(fox_ref[:, sl].astype(F32), lo) * gq_fox
        kn = _pair_rms(fox_ref[:, W_FOX + p * PAIR:W_FOX + (p + 1) * PAIR].astype(F32), lo) * gk_fox
        for hh in range(2):
            h = 2 * p + hh
            qh = qn if hh == 0 else pltpu.roll(qn, HEAD_DIM, 1)
            kh = kn if hh == 0 else pltpu.roll(kn, HEAD_DIM, 1)
            _put_transposed(fqt_out, h, jnp.where(lo, qh, aug[:, h * LANES:(h + 1) * LANES]), TQ)
            ka = aug[:, (N_HEADS_FOX + h) * LANES:(N_HEADS_FOX + h + 1) * LANES]
            fk_out[:, h * LANES:(h + 1) * LANES] = jnp.where(lo, kh, ka).astype(BF16)
        _put_transposed(fvt_out, p, fox_ref[:, 2 * W_FOX + p * PAIR:2 * W_FOX + (p + 1) * PAIR].astype(F32), TK)
        _put_transposed(sqt_out, p, sbq_ref[:, sl].astype(F32) * SCALE, TQ)
        _put_transposed(svt_out, p, sbv_ref[:, sl].astype(F32), TK)

    ang = pos_ref[...].astype(F32) * rope_ref[0:1, :]
    cos = jnp.cos(ang)
    sin_signed = jnp.sin(ang) * rope_ref[1:2, :]
    first_half = rope_ref[2:3, :] > 0.5

    def rope(x):
        partner = jnp.where(first_half, pltpu.roll(x, LANES - HEAD_DIM // 2, 1), pltpu.roll(x, HEAD_DIM // 2, 1))
        return x * cos + partner * sin_signed

    npair_dil = W_DIL // PAIR

    def put_streams(outs):
        for g, (_, dil) in enumerate(DIL_PATTERNS):
            for s in range(dil):
                rows = pl.ds(s, TM // dil, stride=dil) if dil > 1 else slice(None)
                for pp in range(npair_dil):
                    col = s * W_DIL + pp * PAIR
                    outs[g][:, col:col + PAIR] = stage_ref[g * npair_dil + pp, rows, :].astype(BF16)

    for src, gain, outs in ((dq_in, gq_dil, (dq0, dq1, dq2)), (dk_in, gk_dil, (dk0, dk1, dk2))):
        for p in range(W_DIL_ALL // PAIR):
            sl = slice(p * PAIR, (p + 1) * PAIR)
            stage_ref[p] = rope(_pair_rms(src[:, sl].astype(F32), lo) * gain)
        put_streams(outs)
    for p in range(W_DIL_ALL // PAIR):
        stage_ref[p] = dv_in[:, p * PAIR:(p + 1) * PAIR].astype(F32)
    put_streams((dv0, dv1, dv2))


def _aug_tables():
    e = np.zeros((3, LANES, 2 * N_HEADS_FOX * LANES), np.float32)
    ones = np.zeros((1, 2 * N_HEADS_FOX * LANES), np.float32)
    for h in range(N_HEADS_FOX):
        qbase, kbase = h * LANES, (N_HEADS_FOX + h) * LANES
        for part in range(3):
            e[part, h, qbase + AUG_F + part] = 1.0
            ones[0, kbase + AUG_F + part] = 1.0
            e[part, h, kbase + AUG_ONE + part] = -1.0
            ones[0, qbase + AUG_ONE + part] = 1.0
    return jnp.asarray(e, BF16), jnp.asarray(ones, F32)


def _prep(p_fox, flog, p_sb, p_dil, pos, gains, bf, rope_tab, eaug, ones, batch, seq):
    n = p_fox.shape[0]
    nt = seq // TM
    row = lambda b, t: (b * nt + t, 0)
    colblk = lambda j: (lambda b, t: (b * nt + t, j))
    tchunk = lambda heads, w: pl.BlockSpec((None, heads, TM // w, LANES, w), lambda b, t: (b, 0, t, 0, 0))
    tshape = lambda heads, w: jax.ShapeDtypeStruct((batch, heads, seq // w, LANES, w), BF16)
    stream_specs = [pl.BlockSpec((None, TM // d, d * W_DIL), lambda b, t: (b, t, 0)) for _, d in DIL_PATTERNS] * 3
    stream_shapes = [jax.ShapeDtypeStruct((batch, seq // d, d * W_DIL), BF16) for _, d in DIL_PATTERNS] * 3
    npair = W_FOX // PAIR
    return pl.pallas_call(
        _prep_kernel,
        grid=(batch, nt),
        in_specs=[pl.BlockSpec((TM, SZ_FOX_QKV), row), pl.BlockSpec((TM, LANES), row),
                  pl.BlockSpec((TM, W_SB), colblk(0)), pl.BlockSpec((TM, W_SB), colblk(2)),
                  pl.BlockSpec((TM, W_DIL_ALL), colblk(0)), pl.BlockSpec((TM, W_DIL_ALL), colblk(1)),
                  pl.BlockSpec((TM, W_DIL_ALL), colblk(2)),
                  pl.BlockSpec((TM, 1), row),
                  _resident((4, LANES)), _resident((1, LANES)), _resident((3, LANES)),
                  _resident(eaug.shape), _resident(ones.shape)],
        out_specs=[tchunk(N_HEADS_FOX, TQ), pl.BlockSpec((TM, N_HEADS_FOX * LANES), row), tchunk(npair, TK),
                   tchunk(npair, TQ), tchunk(npair, TK)] + stream_specs,
        out_shape=[tshape(N_HEADS_FOX, TQ), jax.ShapeDtypeStruct((n, N_HEADS_FOX * LANES), BF16), tshape(npair, TK),
                   tshape(npair, TQ), tshape(npair, TK)] + stream_shapes,
        scratch_shapes=[pltpu.VMEM((8, LANES), F32), pltpu.VMEM((W_DIL_ALL // PAIR, TM, LANES), F32)],
        compiler_params=_params(("arbitrary", "arbitrary")),
        name="prep",
    )(p_fox, flog, p_sb, p_sb, p_dil, p_dil, p_dil, pos, gains, bf, rope_tab, eaug, ones)


def _chunk_rows(j):
    return pl.ds(j * TK if isinstance(j, int) else pl.multiple_of(j * TK, TK), TK)


def _fox_kernel(qt_ref, k_ref, vt_ref, o_ref, *, seq):
    r = lax.broadcasted_iota(jnp.int32, (TK, TQ), 0)
    c = lax.broadcasted_iota(jnp.int32, (TK, TQ), 1)
    masks = [r + off <= c for off in range(0, TQ, TK)]

    for qi in range(seq // TQ):
        qt = (qt_ref[0, qi], qt_ref[1, qi])

        def scores(j):
            return tuple(_dot(k_ref[_chunk_rows(j), hh * LANES:(hh + 1) * LANES], qt[hh]) for hh in range(2))

        def consume(j, s_pair, state, mask):
            vt = vt_ref[j]
            out = []
            for hh in range(2):
                m_prev, l_prev, acc = state[3 * hh:3 * hh + 3]
                s = s_pair[hh] if mask is None else jnp.where(mask, s_pair[hh], NEG)
                m_new = jnp.maximum(m_prev, jnp.max(s, axis=0, keepdims=True))
                alpha = jnp.exp2(m_prev - m_new)
                p = jnp.exp2(s - m_new)
                l_new = alpha * l_prev + jnp.sum(p, axis=0, keepdims=True)
                acc = alpha * acc + _dot(vt[hh * HEAD_DIM:(hh + 1) * HEAD_DIM, :], p.astype(BF16))
                out += [m_new, l_new, acc]
            return tuple(out)

        n_full = qi * (TQ // TK)
        init = (jnp.full((1, TQ), NEG, F32), jnp.zeros((1, TQ), F32), jnp.zeros((HEAD_DIM, TQ), F32))
        s, st = scores(0), init + init
        if n_full > 0:
            def body(j, carry):
                s_next = scores(j + 1)
                return (*s_next, *consume(j, carry[:2], carry[2:], None))
            carry = lax.fori_loop(0, n_full, body, (*s, *st))
            s, st = carry[:2], carry[2:]
        for d, mask in enumerate(masks):
            s_next = scores(n_full + d + 1) if d + 1 < len(masks) else None
            st = consume(n_full + d, s, st, mask)
            s = s_next
        out_t = jnp.concatenate([st[2] * (1.0 / st[1]), st[5] * (1.0 / st[4])], axis=0)
        o_ref[qi * TQ:(qi + 1) * TQ, :] = jnp.transpose(out_t).astype(BF16)


def _fox_attention(fqt, fk, fvt, batch, seq):
    n = fk.shape[0]
    npair = W_FOX // PAIR
    return pl.pallas_call(
        functools.partial(_fox_kernel, seq=seq),
        grid=(batch, npair),
        in_specs=[pl.BlockSpec((None, 2, seq // TQ, LANES, TQ), lambda b, p: (b, p, 0, 0, 0)),
                  pl.BlockSpec((seq, 2 * LANES), lambda b, p: (b, p)),
                  pl.BlockSpec((None, None, seq // TK, LANES, TK), lambda b, p: (b, p, 0, 0, 0))],
        out_specs=pl.BlockSpec((seq, PAIR), lambda b, p: (b, p)),
        out_shape=jax.ShapeDtypeStruct((n, W_FOX), BF16),
        compiler_params=_params(("arbitrary", "arbitrary")),
        name="fox_attn",
    )(fqt, fk, fvt)


def _sb_kernel(qt_ref, k_ref, vt_ref, o_ref, *, seq):
    lo = _lane_lo()
    r = lax.broadcasted_iota(jnp.int32, (TK, TQ), 0)
    c = lax.broadcasted_iota(jnp.int32, (TK, TQ), 1)
    masks = [r + off < c for off in range(0, TQ, TK)]
    rk = lax.broadcasted_iota(jnp.int32, (TK, TK), 0)
    ck = lax.broadcasted_iota(jnp.int32, (TK, TK), 1)
    tri = jnp.where(rk <= ck, 1.0, 0.0).astype(BF16)
    zero = jnp.zeros((), BF16)

    for qi in range(seq // TQ):
        qt = qt_ref[qi]

        def scores(j):
            kc = k_ref[_chunk_rows(j), :]
            return (_dot(jnp.where(lo, kc, zero), qt), _dot(jnp.where(lo, zero, kc), qt))

        def consume(j, z_pair, state, mask):
            vt = vt_ref[j]
            suffix = []
            for z in z_pair:
                sp = jnp.maximum(z, 0.0) + jnp.log(1.0 + jnp.exp(-jnp.abs(z)))
                if mask is not None:
                    sp = jnp.where(mask, sp, 0.0)
                suffix.append(_dot(tri, sp.astype(BF16)))
            out = []
            for hh in range(2):
                run, acc = state[2 * hh:2 * hh + 2]
                a = jnp.exp(z_pair[hh] - suffix[hh] - run)
                if mask is not None:
                    a = jnp.where(mask, a, 0.0)
                acc = acc + _dot(vt[hh * HEAD_DIM:(hh + 1) * HEAD_DIM, :], a.astype(BF16))
                out += [run + suffix[hh][0:1, :], acc]
            return tuple(out)

        n_full = qi * (TQ // TK)
        init = (jnp.zeros((1, TQ), F32), jnp.zeros((HEAD_DIM, TQ), F32))
        st = init + init
        z = scores(n_full + len(masks) - 1)
        for d in reversed(range(len(masks))):
            j = n_full + d
            z_next = scores(j - 1) if j > 0 else None
            st = consume(j, z, st, masks[d])
            z = z_next
        if n_full > 1:
            def body(i, carry):
                j = n_full - 1 - i
                z_next = scores(j - 1)
                return (*z_next, *consume(j, carry[:2], carry[2:], None))
            carry = lax.fori_loop(0, n_full - 1, body, (*z, *st))
            z, st = carry[:2], carry[2:]
        if n_full > 0:
            st = consume(0, z, st, None)
        out_t = jnp.concatenate([st[1], st[3]], axis=0)
        o_ref[qi * TQ:(qi + 1) * TQ, :] = jnp.transpose(out_t).astype(BF16)


def _sb_attention(sqt, p_sb, svt, batch, seq):
    n = p_sb.shape[0]
    npair = W_SB // PAIR
    tspec = lambda w: pl.BlockSpec((None, None, seq // w, LANES, w), lambda b, p: (b, p, 0, 0, 0))
    return pl.pallas_call(
        functools.partial(_sb_kernel, seq=seq),
        grid=(batch, npair),
        in_specs=[tspec(TQ), pl.BlockSpec((seq, PAIR), lambda b, p: (b, npair + p)), tspec(TK)],
        out_specs=pl.BlockSpec((seq, PAIR), lambda b, p: (b, p)),
        out_shape=jax.ShapeDtypeStruct((n, W_SB), BF16),
        compiler_params=_params(("arbitrary", "arbitrary")),
        name="sb_attn",
    )(sqt, p_sb, svt)


def _dil_kernel(q_ref, k_ref, v_ref, o_ref, lse_ref, *, n):
    lo = _lane_lo()
    r = lax.broadcasted_iota(jnp.int32, (DQ, DQ), 0)
    c = lax.broadcasted_iota(jnp.int32, (DQ, DQ), 1)
    mask_cur = c <= r
    mask_prev = c >= r
    zero = jnp.zeros((), BF16)

    def block(start, has_prev):
        rows = pl.ds(start, DQ)
        prev = pl.ds(start - DQ, DQ)
        for pp in range(W_DIL // PAIR):
            cols = slice(pp * PAIR, (pp + 1) * PAIR)
            q = q_ref[rows, cols]
            k_cur, v_cur = k_ref[rows, cols], v_ref[rows, cols]
            if has_prev:
                k_prev, v_prev = k_ref[prev, cols], v_ref[prev, cols]
            outs, lses = [], []
            for hh in range(2):
                keep = lo if hh == 0 else jnp.logical_not(lo)
                qh = jnp.where(keep, q, zero)
                s_cur = jnp.where(mask_cur, _dot_nt(qh, k_cur), NEG)
                m = jnp.max(s_cur, axis=-1, keepdims=True)
                if has_prev:
                    s_prev = jnp.where(mask_prev, _dot_nt(qh, k_prev), NEG)
                    m = jnp.maximum(m, jnp.max(s_prev, axis=-1, keepdims=True))
                p_cur = jnp.exp(s_cur - m)
                den = jnp.sum(p_cur, axis=-1, keepdims=True)
                o = _dot(p_cur.astype(BF16), jnp.where(keep, v_cur, zero))
                if has_prev:
                    p_prev = jnp.exp(s_prev - m)
                    den = den + jnp.sum(p_prev, axis=-1, keepdims=True)
                    o = o + _dot(p_prev.astype(BF16), jnp.where(keep, v_prev, zero))
                outs.append(o * (1.0 / den))
                lses.append(m + jnp.log(den))
            o_ref[rows, cols] = (outs[0] + outs[1]).astype(BF16)
            lse_ref[rows, cols] = jnp.where(lo, lses[0], lses[1])

    block(0, False)
    if n > DQ:
        def body(i, _):
            block(pl.multiple_of(i * DQ, DQ), True)
            return 0
        lax.fori_loop(1, n // DQ, body, 0)


def _dil_attention(dq, dk, dv, group, batch, seq):
    window, dil = DIL_PATTERNS[group]
    assert window // dil == DQ
    n = seq // dil
    spec = pl.BlockSpec((None, n, W_DIL), lambda b, s: (b, 0, s))
    return pl.pallas_call(
        functools.partial(_dil_kernel, n=n),
        grid=(batch, dil),
        in_specs=[spec, spec, spec],
        out_specs=[spec, spec],
        out_shape=[jax.ShapeDtypeStruct((batch, n, dil * W_DIL), BF16),
                   jax.ShapeDtypeStruct((batch, n, dil * W_DIL), F32)],
        compiler_params=_params(("arbitrary", "arbitrary")),
        name=f"dil_attn_g{group}",
    )(dq, dk, dv)


def _merge_kernel(x_ref, oa_ref, ob_ref, o0_ref, o1_ref, o2_ref, l0_ref, l1_ref, l2_ref, gate_ref,
                  wa_ref, wb_ref, wc_ref, wo_ref, out_ref, stage_ref):
    def tokens(ref, g):
        dil = DIL_PATTERNS[g][1]
        if dil == 1:
            return ref[...].astype(F32)
        npair = W_DIL // PAIR
        for s in range(dil):
            for pp in range(npair):
                col = s * W_DIL + pp * PAIR
                stage_ref[g * npair + pp, pl.ds(s, TM // dil, stride=dil), :] = ref[:, col:col + PAIR].astype(F32)
        return jnp.concatenate([stage_ref[g * npair + pp] for pp in range(npair)], axis=1)

    lses = [tokens(ref, g) for g, ref in enumerate((l0_ref, l1_ref, l2_ref))]
    mx = jnp.maximum(jnp.maximum(lses[0], lses[1]), lses[2])
    es = [jnp.exp(l - mx) for l in lses]
    inv = 1.0 / (es[0] + es[1] + es[2])
    out_c = None
    for g, ref in enumerate((o0_ref, o1_ref, o2_ref)):
        term = (es[g] * inv) * tokens(ref, g)
        out_c = term if out_c is None else out_c + term
    ys = (_dot(oa_ref[...], wa_ref[...]), _dot(ob_ref[...], wb_ref[...]), _dot(out_c.astype(BF16), wc_ref[...]))
    merged = None
    for b in range(N_BRANCHES):
        gate = 1.0 / (1.0 + jnp.exp(-gate_ref[:, b * D_MODEL:(b + 1) * D_MODEL].astype(F32)))
        merged = gate * ys[b] if merged is None else merged + gate * ys[b]
    out_ref[...] = x_ref[...] + _dot(merged.astype(BF16), wo_ref[...])


def _merge(x2, out_a, out_b, os_, lses, gates, wa, wb, wc, wo, seq):
    n = x2.shape[0]
    nt = seq // TM
    row = lambda i: (i, 0)
    tile = lambda w: pl.BlockSpec((TM, w), row)
    streams = [pl.BlockSpec((None, TM // d, d * W_DIL), lambda i: (i // nt, i % nt, 0)) for _, d in DIL_PATTERNS]
    return pl.pallas_call(
        _merge_kernel,
        grid=(n // TM,),
        in_specs=[tile(D_MODEL), tile(W_FOX), tile(W_SB)] + streams * 2 + [tile(SZ_GATES)]
                 + [_resident(w.shape) for w in (wa, wb, wc, wo)],
        out_specs=tile(D_MODEL),
        out_shape=jax.ShapeDtypeStruct((n, D_MODEL), F32),
        scratch_shapes=[pltpu.VMEM((W_DIL_ALL // PAIR, TM, LANES), F32)],
        compiler_params=_params(("arbitrary",)),
        name="merge",
    )(x2, out_a, out_b, *os_, *lses, gates, wa, wb, wc, wo)


def _mlp_kernel(x_ref, g_ref, w1_ref, w2_ref, out_ref):
    x = x_ref[...]
    ms = jnp.mean(x * x, axis=-1, keepdims=True)
    h = (x * lax.rsqrt(ms + EPS) * g_ref[...]).astype(BF16)
    acc = x
    for c in range(0, D_FF, FF_CHUNK):
        u = jnp.maximum(_dot(h, w1_ref[:, c:c + FF_CHUNK]), 0.0)
        acc = acc + _dot((u * u).astype(BF16), w2_ref[c:c + FF_CHUNK, :])
    out_ref[...] = acc


def _mlp(x2, g, w1, w2):
    n = x2.shape[0]
    row = lambda i: (i, 0)
    return pl.pallas_call(
        _mlp_kernel,
        grid=(n // TM,),
        in_specs=[pl.BlockSpec((TM, D_MODEL), row), _resident((1, D_MODEL)),
                  _resident(w1.shape), _resident(w2.shape)],
        out_specs=pl.BlockSpec((TM, D_MODEL), row),
        out_shape=jax.ShapeDtypeStruct((n, D_MODEL), F32),
        compiler_params=_params(("arbitrary",)),
        name="mlp",
    )(x2, g, w1, w2)


def _rope_table():
    half = HEAD_DIM // 2
    lane = np.arange(LANES)
    inv = (1.0 / (ROPE_THETA ** (np.arange(half, dtype=np.float32) / half))).astype(np.float32)
    first = (lane % HEAD_DIM) < half
    return jnp.asarray(np.stack([inv[lane % half], np.where(first, -1.0, 1.0), first.astype(np.float32)]), F32)


def kernel(x, positions, attn_norm, w_in, b_forget, q_norm_fox, k_norm_fox, q_norm_dil, k_norm_dil,
           w_up_fox, w_up_sb, w_up_dil, w_out, mlp_norm, w_mlp_in, w_mlp_out):
    batch, seq, _ = x.shape
    depth = w_in.shape[0]
    n = batch * seq
    o1 = SZ_FOX_QKV
    o2 = o1 + SZ_FORGET
    w_forget = jnp.pad(w_in[:, :, o1:o2], ((0, 0), (0, 0), (0, LANES - SZ_FORGET)))
    w_all = jnp.concatenate([w_in[:, :, :o1], w_in[:, :, o2:], w_forget], axis=2).astype(BF16)
    wa, wb, wc, wo = (w.astype(BF16) for w in (w_up_fox, w_up_sb, w_up_dil, w_out))
    w1, w2 = w_mlp_in.astype(BF16), w_mlp_out.astype(BF16)
    pair = lambda g: jnp.tile(g.astype(F32), (1, LANES // HEAD_DIM))
    gains = jnp.stack([pair(q_norm_fox) * (SCALE * LOG2E), pair(k_norm_fox),
                       pair(q_norm_dil) * SCALE, pair(k_norm_dil)], axis=1)
    bf = jnp.pad(b_forget.astype(F32), ((0, 0), (0, LANES - SZ_FORGET)))[:, None, :]
    rope_tab = _rope_table()
    eaug, ones = _aug_tables()
    pos = positions.reshape(n, 1)

    x2 = x.reshape(n, D_MODEL)
    for l in range(depth):
        p_fox, p_sb, p_dil, gates, flog = _inproj(x2, attn_norm[l][None, :], w_all[l])
        prep = _prep(p_fox, flog, p_sb, p_dil, pos, gains[l], bf[l], rope_tab, eaug, ones, batch, seq)
        fqt, fk, fvt, sqt, svt = prep[:5]
        dqs, dks, dvs = prep[5:8], prep[8:11], prep[11:14]
        out_a = _fox_attention(fqt, fk, fvt, batch, seq)
        out_b = _sb_attention(sqt, p_sb, svt, batch, seq)
        dil = [_dil_attention(dqs[g], dks[g], dvs[g], g, batch, seq) for g in range(N_DIL_GROUPS)]
        x2 = _merge(x2, out_a, out_b, [d[0] for d in dil], [d[1] for d in dil], gates,
                    wa[l], wb[l], wc[l], wo[l], seq)
        x2 = _mlp(x2, mlp_norm[l][None, :], w1[l], w2[l])
    return x2.reshape(batch, seq, D_MODEL)
```

```python
import functools
import math

import numpy as np
import jax
import jax.numpy as jnp
from jax import lax
from jax.experimental import pallas as pl
from jax.experimental.pallas import tpu as pltpu

D_MODEL = 1024
HEAD_DIM = 64
N_HEADS_FOX = 8
N_HEADS_SB = 8
DIL_PATTERNS = ((128, 1), (512, 4), (2048, 16))
N_DIL_GROUPS = len(DIL_PATTERNS)
N_HEADS_DIL = 4
D_FF = 4 * D_MODEL
ROPE_THETA = 10000.0
EPS = 1e-6
N_BRANCHES = 3

W_FOX = N_HEADS_FOX * HEAD_DIM
W_SB = N_HEADS_SB * HEAD_DIM
W_DIL = N_HEADS_DIL * HEAD_DIM
W_DIL_ALL = N_DIL_GROUPS * W_DIL
SZ_FOX_QKV = 3 * W_FOX
SZ_FORGET = N_HEADS_FOX
SZ_SB_QKV = 3 * W_SB
SZ_DIL_QKV = 3 * W_DIL_ALL
SZ_GATES = N_BRANCHES * D_MODEL

LANES = 128
PAIR = LANES
SCALE = 1.0 / math.sqrt(HEAD_DIM)
LOG2E = math.log2(math.e)
NEG = -1e30

TM = 512
TQ = 512
TK = 256
DQ = 128
PROJ_CHUNK = 768
FF_CHUNK = 1024
VMEM_LIMIT = 56 * 1024 * 1024

AUG_F = HEAD_DIM
AUG_ONE = HEAD_DIM + 3

F32 = jnp.float32
BF16 = jnp.bfloat16


def _params(sem):
    return pltpu.CompilerParams(dimension_semantics=sem, vmem_limit_bytes=VMEM_LIMIT)


def _resident(shape):
    nd = len(shape)
    return pl.BlockSpec(shape, lambda *_: (0,) * nd, pipeline_mode=pl.Buffered(1))


def _dot(a, b):
    return jnp.dot(a, b, preferred_element_type=F32)


def _dot_nt(a, b):
    return lax.dot_general(a, b, (((1,), (1,)), ((), ())), preferred_element_type=F32)


def _split3(x):
    hi = x.astype(BF16)
    r1 = x - hi.astype(F32)
    mid = r1.astype(BF16)
    lo = (r1 - mid.astype(F32)).astype(BF16)
    return hi, mid, lo


def _lane_lo():
    return lax.broadcasted_iota(jnp.int32, (1, LANES), 1) < HEAD_DIM


def _inproj_kernel(x_ref, g_ref, w_ref, fox_ref, sb_ref, dil_ref, gate_ref, flog_ref):
    x = x_ref[...]
    ms = jnp.mean(x * x, axis=-1, keepdims=True)
    h = (x * lax.rsqrt(ms + EPS) * g_ref[...]).astype(BF16)
    col = 0
    for out_ref, width in ((fox_ref, SZ_FOX_QKV), (sb_ref, SZ_SB_QKV), (dil_ref, SZ_DIL_QKV), (gate_ref, SZ_GATES)):
        for c in range(0, width, PROJ_CHUNK):
            out_ref[:, c:c + PROJ_CHUNK] = _dot(h, w_ref[:, col + c:col + c + PROJ_CHUNK]).astype(BF16)
        col += width
    flog_ref[...] = _dot(h, w_ref[:, col:col + LANES])


def _inproj(x2, g, w):
    n = x2.shape[0]
    wcols = w.shape[1]
    row = lambda i: (i, 0)
    return pl.pallas_call(
        _inproj_kernel,
        grid=(n // TM,),
        in_specs=[pl.BlockSpec((TM, D_MODEL), row), _resident((1, D_MODEL)), _resident((D_MODEL, wcols))],
        out_specs=[pl.BlockSpec((TM, SZ_FOX_QKV), row), pl.BlockSpec((TM, SZ_SB_QKV), row),
                   pl.BlockSpec((TM, SZ_DIL_QKV), row), pl.BlockSpec((TM, SZ_GATES), row),
                   pl.BlockSpec((TM, LANES), row)],
        out_shape=[jax.ShapeDtypeStruct((n, SZ_FOX_QKV), BF16), jax.ShapeDtypeStruct((n, SZ_SB_QKV), BF16),
                   jax.ShapeDtypeStruct((n, SZ_DIL_QKV), BF16), jax.ShapeDtypeStruct((n, SZ_GATES), BF16),
                   jax.ShapeDtypeStruct((n, LANES), F32)],
        compiler_params=_params(("arbitrary",)),
        name="inproj",
    )(x2, g, w)


def _pair_rms(x, lo):
    x2 = x * x
    s_lo = jnp.sum(jnp.where(lo, x2, 0.0), axis=-1, keepdims=True)
    s_hi = jnp.sum(jnp.where(lo, 0.0, x2), axis=-1, keepdims=True)
    r = jnp.where(lo, lax.rsqrt(s_lo * (1.0 / HEAD_DIM) + EPS), lax.rsqrt(s_hi * (1.0 / HEAD_DIM) + EPS))
    return x * r


def _put_transposed(out_ref, idx, tile, width):
    tt = jnp.transpose(tile).astype(BF16)
    for i in range(TM // width):
        out_ref[idx, i] = tt[:, i * width:(i + 1) * width]


def _prep_kernel(fox_ref, flog_ref, sbq_ref, sbv_ref, dq_in, dk_in, dv_in, pos_ref, gains_ref, bf_ref,
                 rope_ref, invf_ref, tri_ref, eaug_ref, ones_ref,
                 fqt_out, fk_out, fvt_out, sqt_out, svt_out,
                 dq0, dq1, dq2, dk0, dk1, dk2, dv0, dv1, dv2, carry_ref, stage_ref):
    t = pl.program_id(1)
    lo = _lane_lo()
    gq_fox, gk_fox, gq_dil, gk_dil = (gains_ref[i:i + 1, :] for i in range(4))

    @pl.when(t == 0)
    def _():
        carry_ref[...] = jnp.zeros_like(carry_ref)

    z = flog_ref[...] + bf_ref[...]
    log_f = jnp.minimum(z, 0.0) - jnp.log(1.0 + jnp.exp(-jnp.abs(z)))
    tri = tri_ref[...]
    hi, mid, lw = _split3(log_f)
    f_cum = (_dot(tri, hi) + _dot(tri, mid) + _dot(tri, lw)) + carry_ref[0:1, :]
    carry_ref[...] = jnp.broadcast_to(f_cum[TM - 1:TM, :], carry_ref.shape)
    hi, mid, lw = _split3(f_cum * LOG2E)
    aug = _dot(hi, eaug_ref[0]) + _dot(mid, eaug_ref[1]) + _dot(lw, eaug_ref[2]) + ones_ref[...]

    for p in range(W_FOX // PAIR):
        sl = slice(p * PAIR, (p + 1) * PAIR)
        qn = _pair_rms(fox_ref[:, sl].astype(F32), lo) * gq_fox
        kn = _pair_rms(fox_ref[:, W_FOX + p * PAIR:W_FOX + (p + 1) * PAIR].astype(F32), lo) * gk_fox
        for hh in range(2):
            h = 2 * p + hh
            qh = qn if hh == 0 else pltpu.roll(qn, HEAD_DIM, 1)
            kh = kn if hh == 0 else pltpu.roll(kn, HEAD_DIM, 1)
            _put_transposed(fqt_out, h, jnp.where(lo, qh, aug[:, h * LANES:(h + 1) * LANES]), TQ)
            ka = aug[:, (N_HEADS_FOX + h) * LANES:(N_HEADS_FOX + h + 1) * LANES]
            fk_out[:, h * LANES:(h + 1) * LANES] = jnp.where(lo, kh, ka).astype(BF16)
        _put_transposed(fvt_out, p, fox_ref[:, 2 * W_FOX + p * PAIR:2 * W_FOX + (p + 1) * PAIR].astype(F32), TK)
        _put_transposed(sqt_out, p, sbq_ref[:, sl].astype(F32) * (SCALE * LOG2E), TQ)
        _put_transposed(svt_out, p, sbv_ref[:, sl].astype(F32), TK)

    ang_t = pos_ref[...].astype(F32) * invf_ref[...]
    reps = LANES // (HEAD_DIM // 2)
    cos = jnp.transpose(jnp.concatenate([jnp.cos(ang_t)] * reps, axis=0))
    sin_signed = jnp.transpose(jnp.concatenate([jnp.sin(ang_t)] * reps, axis=0)) * rope_ref[0:1, :]
    first_half = rope_ref[1:2, :] > 0.5

    def rope(x):
        partner = jnp.where(first_half, pltpu.roll(x, LANES - HEAD_DIM // 2, 1), pltpu.roll(x, HEAD_DIM // 2, 1))
        return x * cos + partner * sin_signed

    npair_dil = W_DIL // PAIR

    def put_streams(outs):
        for g, (_, dil) in enumerate(DIL_PATTERNS):
            for s in range(dil):
                rows = pl.ds(s, TM // dil, stride=dil) if dil > 1 else slice(None)
                for pp in range(npair_dil):
                    col = s * W_DIL + pp * PAIR
                    outs[g][:, col:col + PAIR] = stage_ref[g * npair_dil + pp, rows, :].astype(BF16)

    for src, gain, outs in ((dq_in, gq_dil, (dq0, dq1, dq2)), (dk_in, gk_dil, (dk0, dk1, dk2))):
        for p in range(W_DIL_ALL // PAIR):
            sl = slice(p * PAIR, (p + 1) * PAIR)
            stage_ref[p] = rope(_pair_rms(src[:, sl].astype(F32), lo) * gain)
        put_streams(outs)
    for p in range(W_DIL_ALL // PAIR):
        stage_ref[p] = dv_in[:, p * PAIR:(p + 1) * PAIR].astype(F32)
    put_streams((dv0, dv1, dv2))


def _aug_tables():
    e = np.zeros((3, LANES, 2 * N_HEADS_FOX * LANES), np.float32)
    ones = np.zeros((1, 2 * N_HEADS_FOX * LANES), np.float32)
    for h in range(N_HEADS_FOX):
        qbase, kbase = h * LANES, (N_HEADS_FOX + h) * LANES
        for part in range(3):
            e[part, h, qbase + AUG_F + part] = 1.0
            ones[0, kbase + AUG_F + part] = 1.0
            e[part, h, kbase + AUG_ONE + part] = -1.0
            ones[0, qbase + AUG_ONE + part] = 1.0
    return jnp.asarray(e, BF16), jnp.asarray(ones, F32)


def _prep(p_fox, flog, p_sb, p_dil, pos, gains, bf, tables, batch, seq):
    rope_tab, invf, tri, eaug, ones = tables
    n = p_fox.shape[0]
    nt = seq // TM
    row = lambda b, t: (b * nt + t, 0)
    colblk = lambda j: (lambda b, t: (b * nt + t, j))
    tchunk = lambda heads, w: pl.BlockSpec((None, heads, TM // w, LANES, w), lambda b, t: (b, 0, t, 0, 0))
    tshape = lambda heads, w: jax.ShapeDtypeStruct((batch, heads, seq // w, LANES, w), BF16)
    stream_specs = [pl.BlockSpec((None, TM // d, d * W_DIL), lambda b, t: (b, t, 0)) for _, d in DIL_PATTERNS] * 3
    stream_shapes = [jax.ShapeDtypeStruct((batch, seq // d, d * W_DIL), BF16) for _, d in DIL_PATTERNS] * 3
    npair = W_FOX // PAIR
    return pl.pallas_call(
        _prep_kernel,
        grid=(batch, nt),
        in_specs=[pl.BlockSpec((TM, SZ_FOX_QKV), row), pl.BlockSpec((TM, LANES), row),
                  pl.BlockSpec((TM, W_SB), colblk(0)), pl.BlockSpec((TM, W_SB), colblk(2)),
                  pl.BlockSpec((TM, W_DIL_ALL), colblk(0)), pl.BlockSpec((TM, W_DIL_ALL), colblk(1)),
                  pl.BlockSpec((TM, W_DIL_ALL), colblk(2)),
                  pl.BlockSpec((None, 1, TM), lambda b, t: (b * nt + t, 0, 0)),
                  _resident((4, LANES)), _resident((1, LANES))]
                 + [_resident(a.shape) for a in (rope_tab, invf, tri, eaug, ones)],
        out_specs=[tchunk(N_HEADS_FOX, TQ), pl.BlockSpec((TM, N_HEADS_FOX * LANES), row), tchunk(npair, TK),
                   tchunk(npair, TQ), tchunk(npair, TK)] + stream_specs,
        out_shape=[tshape(N_HEADS_FOX, TQ), jax.ShapeDtypeStruct((n, N_HEADS_FOX * LANES), BF16), tshape(npair, TK),
                   tshape(npair, TQ), tshape(npair, TK)] + stream_shapes,
        scratch_shapes=[pltpu.VMEM((8, LANES), F32), pltpu.VMEM((W_DIL_ALL // PAIR, TM, LANES), F32)],
        compiler_params=_params(("arbitrary", "arbitrary")),
        name="prep",
    )(p_fox, flog, p_sb, p_sb, p_dil, p_dil, p_dil, pos, gains, bf, *tables)


def _chunk_rows(j):
    return pl.ds(j * TK if isinstance(j, int) else pl.multiple_of(j * TK, TK), TK)


def _attn_scratch():
    return [pltpu.VMEM((2, TK, TQ), F32), pltpu.VMEM((2, TK, TQ), F32), pltpu.VMEM((2, HEAD_DIM, TQ), F32)]


def _fox_kernel(qt_ref, k_ref, vt_ref, o_ref, sa_ref, sb_ref, acc_ref, *, seq):
    assert TQ == 2 * TK
    r = lax.broadcasted_iota(jnp.int32, (TK, TQ), 0)
    c = lax.broadcasted_iota(jnp.int32, (TK, TQ), 1)
    masks = [r + off <= c for off in range(0, TQ, TK)]

    for qi in range(seq // TQ):
        qt = (qt_ref[0, qi], qt_ref[1, qi])

        def produce(j, buf):
            for hh in range(2):
                buf[hh] = _dot(k_ref[_chunk_rows(j), hh * LANES:(hh + 1) * LANES], qt[hh])

        def consume(j, buf, state, mask):
            vt = vt_ref[j]
            out = []
            for hh in range(2):
                m_prev, l_prev = state[2 * hh:2 * hh + 2]
                s = buf[hh] if mask is None else jnp.where(mask, buf[hh], NEG)
                m_new = jnp.maximum(m_prev, jnp.max(s, axis=0, keepdims=True))
                alpha = jnp.exp2(m_prev - m_new)
                p = jnp.exp2(s - m_new)
                l_new = alpha * l_prev + jnp.sum(p, axis=0, keepdims=True)
                acc_ref[hh] = alpha * acc_ref[hh] + _dot(vt[hh * HEAD_DIM:(hh + 1) * HEAD_DIM, :], p.astype(BF16))
                out += [m_new, l_new]
            return tuple(out)

        n_full = qi * (TQ // TK)
        acc_ref[...] = jnp.zeros_like(acc_ref)
        init = (jnp.full((1, TQ), NEG, F32), jnp.zeros((1, TQ), F32))
        st = init + init
        produce(0, sa_ref)
        if n_full > 0:
            def body(i, st):
                j = 2 * i
                produce(j + 1, sb_ref)
                st = consume(j, sa_ref, st, None)
                produce(j + 2, sa_ref)
                return consume(j + 1, sb_ref, st, None)
            st = lax.fori_loop(0, n_full // 2, body, st)
        produce(n_full + 1, sb_ref)
        st = consume(n_full, sa_ref, st, masks[0])
        st = consume(n_full + 1, sb_ref, st, masks[1])
        out_t = jnp.concatenate([acc_ref[0] * (1.0 / st[1]), acc_ref[1] * (1.0 / st[3])], axis=0)
        o_ref[qi * TQ:(qi + 1) * TQ, :] = jnp.transpose(out_t).astype(BF16)


def _fox_attention(fqt, fk, fvt, batch, seq):
    n = fk.shape[0]
    npair = W_FOX // PAIR
    return pl.pallas_call(
        functools.partial(_fox_kernel, seq=seq),
        grid=(batch, npair),
        in_specs=[pl.BlockSpec((None, 2, seq // TQ, LANES, TQ), lambda b, p: (b, p, 0, 0, 0)),
                  pl.BlockSpec((seq, 2 * LANES), lambda b, p: (b, p)),
                  pl.BlockSpec((None, None, seq // TK, LANES, TK), lambda b, p: (b, p, 0, 0, 0))],
        out_specs=pl.BlockSpec((seq, PAIR), lambda b, p: (b, p)),
        out_shape=jax.ShapeDtypeStruct((n, W_FOX), BF16),
        scratch_shapes=_attn_scratch(),
        compiler_params=_params(("arbitrary", "arbitrary")),
        name="fox_attn",
    )(fqt, fk, fvt)


def _sb_kernel(qt_ref, k_ref, vt_ref, o_ref, za_ref, zb_ref, acc_ref, *, seq):
    assert TQ == 2 * TK
    lo = _lane_lo()
    r = lax.broadcasted_iota(jnp.int32, (TK, TQ), 0)
    c = lax.broadcasted_iota(jnp.int32, (TK, TQ), 1)
    masks = [r + off < c for off in range(0, TQ, TK)]
    rk = lax.broadcasted_iota(jnp.int32, (TK, TK), 0)
    ck = lax.broadcasted_iota(jnp.int32, (TK, TK), 1)
    tri = jnp.where(rk <= ck, 1.0, 0.0).astype(BF16)
    zero = jnp.zeros((), BF16)

    for qi in range(seq // TQ):
        qt = qt_ref[qi]

        def produce(j, buf):
            kc = k_ref[_chunk_rows(j), :]
            buf[0] = _dot(jnp.where(lo, kc, zero), qt)
            buf[1] = _dot(jnp.where(lo, zero, kc), qt)

        def consume(j, buf, runs, mask):
            vt = vt_ref[j]
            suffix = []
            for hh in range(2):
                z = buf[hh]
                sp = jnp.maximum(z, 0.0) + jnp.log(1.0 + jnp.exp2(-jnp.abs(z))) * LOG2E
                if mask is not None:
                    sp = jnp.where(mask, sp, 0.0)
                suffix.append(_dot(tri, sp.astype(BF16)))
            out = []
            for hh in range(2):
                a = jnp.exp2(buf[hh] - suffix[hh] - runs[hh])
                if mask is not None:
                    a = jnp.where(mask, a, 0.0)
                acc_ref[hh] += _dot(vt[hh * HEAD_DIM:(hh + 1) * HEAD_DIM, :], a.astype(BF16))
                out.append(runs[hh] + suffix[hh][0:1, :])
            return tuple(out)

        n_full = qi * (TQ // TK)
        acc_ref[...] = jnp.zeros_like(acc_ref)
        runs = (jnp.zeros((1, TQ), F32),) * 2
        produce(n_full + 1, zb_ref)
        produce(n_full, za_ref)
        runs = consume(n_full + 1, zb_ref, runs, masks[1])
        if n_full > 0:
            produce(n_full - 1, zb_ref)
        runs = consume(n_full, za_ref, runs, masks[0])
        if n_full > 0:
            def body(i, runs):
                j = n_full - 1 - 2 * i
                produce(j - 1, za_ref)
                runs = consume(j, zb_ref, runs, None)
                produce(j - 2, zb_ref)
                return consume(j - 1, za_ref, runs, None)
            if n_full > 2:
                runs = lax.fori_loop(0, n_full // 2 - 1, body, runs)
            produce(0, za_ref)
            runs = consume(1, zb_ref, runs, None)
            runs = consume(0, za_ref, runs, None)
        out_t = jnp.concatenate([acc_ref[0], acc_ref[1]], axis=0)
        o_ref[qi * TQ:(qi + 1) * TQ, :] = jnp.transpose(out_t).astype(BF16)


def _sb_attention(sqt, p_sb, svt, batch, seq):
    n = p_sb.shape[0]
    npair = W_SB // PAIR
    tspec = lambda w: pl.BlockSpec((None, None, seq // w, LANES, w), lambda b, p: (b, p, 0, 0, 0))
    return pl.pallas_call(
        functools.partial(_sb_kernel, seq=seq),
        grid=(batch, npair),
        in_specs=[tspec(TQ), pl.BlockSpec((seq, PAIR), lambda b, p: (b, npair + p)), tspec(TK)],
        out_specs=pl.BlockSpec((seq, PAIR), lambda b, p: (b, p)),
        out_shape=jax.ShapeDtypeStruct((n, W_SB), BF16),
        scratch_shapes=_attn_scratch(),
        compiler_params=_params(("arbitrary", "arbitrary")),
        name="sb_attn",
    )(sqt, p_sb, svt)


def _dil_kernel(q_ref, k_ref, v_ref, o_ref, lse_ref, *, n, dil):
    lo = _lane_lo()
    keeps = (lo, jnp.logical_not(lo))
    r = lax.broadcasted_iota(jnp.int32, (DQ, 2 * DQ), 0)
    c = lax.broadcasted_iota(jnp.int32, (DQ, 2 * DQ), 1)
    band = (c >= r) & (c <= r + DQ)
    causal = (lax.broadcasted_iota(jnp.int32, (DQ, DQ), 1)
              <= lax.broadcasted_iota(jnp.int32, (DQ, DQ), 0))
    zero = jnp.zeros((), BF16)
    npair = W_DIL // PAIR

    def block(col0, start, has_prev):
        rows = pl.ds(start, DQ)
        krows = pl.ds(start - DQ, 2 * DQ) if has_prev else rows
        mask = band if has_prev else causal
        vs, ss = [], []
        for pp in range(npair):
            cols = slice(col0 + pp * PAIR, col0 + (pp + 1) * PAIR)
            q, k = q_ref[rows, cols], k_ref[krows, cols]
            vs.append(v_ref[krows, cols])
            ss.append([_dot_nt(jnp.where(keep, q, zero), k) for keep in keeps])
        for pp in range(npair):
            cols = slice(col0 + pp * PAIR, col0 + (pp + 1) * PAIR)
            outs, lses = [], []
            for hh, keep in enumerate(keeps):
                s = jnp.where(mask, ss[pp][hh], NEG)
                m = jnp.max(s, axis=-1, keepdims=True)
                p = jnp.exp(s - m)
                den = jnp.sum(p, axis=-1, keepdims=True)
                o = _dot(p.astype(BF16), jnp.where(keep, vs[pp], zero))
                outs.append(o * (1.0 / den))
                lses.append(m + jnp.log(den))
            o_ref[rows, cols] = (outs[0] + outs[1]).astype(BF16)
            lse_ref[rows, cols] = jnp.where(lo, lses[0], lses[1])

    for stream in range(dil):
        col0 = stream * W_DIL
        block(col0, 0, False)
        if n > DQ:
            def body(i, _):
                block(col0, pl.multiple_of(i * DQ, DQ), True)
                return 0
            lax.fori_loop(1, n // DQ, body, 0)


def _dil_attention(dq, dk, dv, group, batch, seq):
    window, dil = DIL_PATTERNS[group]
    assert window // dil == DQ
    n = seq // dil
    spec = pl.BlockSpec((None, n, dil * W_DIL), lambda b: (b, 0, 0))
    return pl.pallas_call(
        functools.partial(_dil_kernel, n=n, dil=dil),
        grid=(batch,),
        in_specs=[spec, spec, spec],
        out_specs=[spec, spec],
        out_shape=[jax.ShapeDtypeStruct((batch, n, dil * W_DIL), BF16),
                   jax.ShapeDtypeStruct((batch, n, dil * W_DIL), F32)],
        compiler_params=_params(("arbitrary",)),
        name=f"dil_attn_g{group}",
    )(dq, dk, dv)


def _merge_kernel(x_ref, oa_ref, ob_ref, o0_ref, o1_ref, o2_ref, l0_ref, l1_ref, l2_ref, gate_ref,
                  wa_ref, wb_ref, wc_ref, wo_ref, out_ref, stage_ref):
    def tokens(ref, g):
        dil = DIL_PATTERNS[g][1]
        if dil == 1:
            return ref[...].astype(F32)
        npair = W_DIL // PAIR
        for s in range(dil):
            for pp in range(npair):
                col = s * W_DIL + pp * PAIR
                stage_ref[g * npair + pp, pl.ds(s, TM // dil, stride=dil), :] = ref[:, col:col + PAIR].astype(F32)
        return jnp.concatenate([stage_ref[g * npair + pp] for pp in range(npair)], axis=1)

    lses = [tokens(ref, g) for g, ref in enumerate((l0_ref, l1_ref, l2_ref))]
    mx = jnp.maximum(jnp.maximum(lses[0], lses[1]), lses[2])
    es = [jnp.exp(l - mx) for l in lses]
    inv = 1.0 / (es[0] + es[1] + es[2])
    out_c = None
    for g, ref in enumerate((o0_ref, o1_ref, o2_ref)):
        term = (es[g] * inv) * tokens(ref, g)
        out_c = term if out_c is None else out_c + term
    ys = (_dot(oa_ref[...], wa_ref[...]), _dot(ob_ref[...], wb_ref[...]), _dot(out_c.astype(BF16), wc_ref[...]))
    merged = None
    for b in range(N_BRANCHES):
        gate = 1.0 / (1.0 + jnp.exp(-gate_ref[:, b * D_MODEL:(b + 1) * D_MODEL].astype(F32)))
        merged = gate * ys[b] if merged is None else merged + gate * ys[b]
    out_ref[...] = x_ref[...] + _dot(merged.astype(BF16), wo_ref[...])


def _merge(x2, out_a, out_b, os_, lses, gates, wa, wb, wc, wo, seq):
    n = x2.shape[0]
    nt = seq // TM
    row = lambda i: (i, 0)
    tile = lambda w: pl.BlockSpec((TM, w), row)
    streams = [pl.BlockSpec((None, TM // d, d * W_DIL), lambda i: (i // nt, i % nt, 0)) for _, d in DIL_PATTERNS]
    return pl.pallas_call(
        _merge_kernel,
        grid=(n // TM,),
        in_specs=[tile(D_MODEL), tile(W_FOX), tile(W_SB)] + streams * 2 + [tile(SZ_GATES)]
                 + [_resident(w.shape) for w in (wa, wb, wc, wo)],
        out_specs=tile(D_MODEL),
        out_shape=jax.ShapeDtypeStruct((n, D_MODEL), F32),
        scratch_shapes=[pltpu.VMEM((W_DIL_ALL // PAIR, TM, LANES), F32)],
        compiler_params=_params(("arbitrary",)),
        name="merge",
    )(x2, out_a, out_b, *os_, *lses, gates, wa, wb, wc, wo)


def _mlp_kernel(x_ref, g_ref, w1_ref, w2_ref, out_ref):
    x = x_ref[...]
    ms = jnp.mean(x * x, axis=-1, keepdims=True)
    h = (x * lax.rsqrt(ms + EPS) * g_ref[...]).astype(BF16)
    acc = x
    for c in range(0, D_FF, FF_CHUNK):
        u = jnp.maximum(_dot(h, w1_ref[:, c:c + FF_CHUNK]), 0.0)
        acc = acc + _dot((u * u).astype(BF16), w2_ref[c:c + FF_CHUNK, :])
    out_ref[...] = acc


def _mlp(x2, g, w1, w2):
    n = x2.shape[0]
    row = lambda i: (i, 0)
    return pl.pallas_call(
        _mlp_kernel,
        grid=(n // TM,),
        in_specs=[pl.BlockSpec((TM, D_MODEL), row), _resident((1, D_MODEL)),
                  _resident(w1.shape), _resident(w2.shape)],
        out_specs=pl.BlockSpec((TM, D_MODEL), row),
        out_shape=jax.ShapeDtypeStruct((n, D_MODEL), F32),
        compiler_params=_params(("arbitrary",)),
        name="mlp",
    )(x2, g, w1, w2)


def _prep_tables():
    half = HEAD_DIM // 2
    lane = np.arange(LANES)
    inv = (1.0 / (ROPE_THETA ** (np.arange(half, dtype=np.float32) / half))).astype(np.float32)
    first = (lane % HEAD_DIM) < half
    rope_tab = jnp.asarray(np.stack([np.where(first, -1.0, 1.0), first.astype(np.float32)]), F32)
    invf = jnp.asarray(np.repeat(inv[:, None], TM, axis=1), F32)
    tri = jnp.asarray(np.tril(np.ones((TM, TM), np.float32)), BF16)
    return (rope_tab, invf, tri) + _aug_tables()


def kernel(x, positions, attn_norm, w_in, b_forget, q_norm_fox, k_norm_fox, q_norm_dil, k_norm_dil,
           w_up_fox, w_up_sb, w_up_dil, w_out, mlp_norm, w_mlp_in, w_mlp_out):
    batch, seq, _ = x.shape
    depth = w_in.shape[0]
    n = batch * seq
    o1 = SZ_FOX_QKV
    o2 = o1 + SZ_FORGET
    w_forget = jnp.pad(w_in[:, :, o1:o2], ((0, 0), (0, 0), (0, LANES - SZ_FORGET)))
    w_all = jnp.concatenate([w_in[:, :, :o1], w_in[:, :, o2:], w_forget], axis=2).astype(BF16)
    wa, wb, wc, wo = (w.astype(BF16) for w in (w_up_fox, w_up_sb, w_up_dil, w_out))
    w1, w2 = w_mlp_in.astype(BF16), w_mlp_out.astype(BF16)
    pair = lambda g: jnp.tile(g.astype(F32), (1, LANES // HEAD_DIM))
    gains = jnp.stack([pair(q_norm_fox) * (SCALE * LOG2E), pair(k_norm_fox),
                       pair(q_norm_dil) * SCALE, pair(k_norm_dil)], axis=1)
    bf = jnp.pad(b_forget.astype(F32), ((0, 0), (0, LANES - SZ_FORGET)))[:, None, :]
    tables = _prep_tables()
    pos = positions.reshape(n // TM, 1, TM)

    x2 = x.reshape(n, D_MODEL)
    for l in range(depth):
        p_fox, p_sb, p_dil, gates, flog = _inproj(x2, attn_norm[l][None, :], w_all[l])
        prep = _prep(p_fox, flog, p_sb, p_dil, pos, gains[l], bf[l], tables, batch, seq)
        fqt, fk, fvt, sqt, svt = prep[:5]
        dqs, dks, dvs = prep[5:8], prep[8:11], prep[11:14]
        out_a = _fox_attention(fqt, fk, fvt, batch, seq)
        out_b = _sb_attention(sqt, p_sb, svt, batch, seq)
        dil = [_dil_attention(dqs[g], dks[g], dvs[g], g, batch, seq) for g in range(N_DIL_GROUPS)]
        x2 = _merge(x2, out_a, out_b, [d[0] for d in dil], [d[1] for d in dil], gates,
                    wa[l], wb[l], wc[l], wo[l], seq)
        x2 = _mlp(x2, mlp_norm[l][None, :], w1[l], w2[l])
    return x2.reshape(batch, seq, D_MODEL)
```

```python
import functools
import math

import numpy as np
import jax
import jax.numpy as jnp
from jax import lax
from jax.experimental import pallas as pl
from jax.experimental.pallas import tpu as pltpu

D_MODEL = 1024
HEAD_DIM = 64
N_HEADS_FOX = 8
N_HEADS_SB = 8
DIL_PATTERNS = ((128, 1), (512, 4), (2048, 16))
N_DIL_GROUPS = len(DIL_PATTERNS)
N_HEADS_DIL = 4
D_FF = 4 * D_MODEL
ROPE_THETA = 10000.0
EPS = 1e-6
N_BRANCHES = 3

W_FOX = N_HEADS_FOX * HEAD_DIM
W_SB = N_HEADS_SB * HEAD_DIM
W_DIL = N_HEADS_DIL * HEAD_DIM
W_DIL_ALL = N_DIL_GROUPS * W_DIL
SZ_FOX_QKV = 3 * W_FOX
SZ_FORGET = N_HEADS_FOX
SZ_SB_QKV = 3 * W_SB
SZ_DIL_QKV = 3 * W_DIL_ALL
SZ_GATES = N_BRANCHES * D_MODEL

LANES = 128
PAIR = LANES
SCALE = 1.0 / math.sqrt(HEAD_DIM)
LOG2E = math.log2(math.e)
NEG = -1e30

TM = 512
TQ = 512
TK = 256
DQ = 128
PROJ_CHUNK = 768
FF_CHUNK = 1024
VMEM_LIMIT = 56 * 1024 * 1024

AUG_F = 0
AUG_ONE = 3

F32 = jnp.float32
BF16 = jnp.bfloat16


def _params(sem):
    return pltpu.CompilerParams(dimension_semantics=sem, vmem_limit_bytes=VMEM_LIMIT)


def _resident(shape):
    nd = len(shape)
    return pl.BlockSpec(shape, lambda *_: (0,) * nd, pipeline_mode=pl.Buffered(1))


def _dot(a, b):
    return jnp.dot(a, b, preferred_element_type=F32)


def _dot_nt(a, b):
    return lax.dot_general(a, b, (((1,), (1,)), ((), ())), preferred_element_type=F32)


def _split3(x):
    hi = x.astype(BF16)
    r1 = x - hi.astype(F32)
    mid = r1.astype(BF16)
    lo = (r1 - mid.astype(F32)).astype(BF16)
    return hi, mid, lo


def _lane_lo():
    return lax.broadcasted_iota(jnp.int32, (1, LANES), 1) < HEAD_DIM


OFF_FOX = 0
OFF_SB = OFF_FOX + SZ_FOX_QKV
OFF_DIL = OFF_SB + SZ_SB_QKV
OFF_GATES = OFF_DIL + SZ_DIL_QKV
OFF_FORGET = OFF_GATES + SZ_GATES
MXU_COLS = 256


def _head_rms(y, head_sum):
    sq = (y * y).astype(BF16)
    ss = jnp.concatenate([_dot(sq[:, c:c + MXU_COLS], head_sum) for c in range(0, y.shape[1], MXU_COLS)], axis=1)
    return y * lax.rsqrt(ss * (1.0 / HEAD_DIM) + EPS)


def _put_transposed(out_ref, idx, tile, width):
    tt = jnp.transpose(tile).astype(BF16)
    for i in range(TM // width):
        out_ref[idx, i] = tt[:, i * width:(i + 1) * width]


def _front_kernel(x_ref, g_ref, w_ref, pos_ref, gains_ref, bf_ref, rope_ref, invf_ref, tri_ref, eaug_ref, ones_ref,
                  fqt_out, fk_out, fvt_out, sqt_out, sbk_out, svt_out,
                  dq0, dq1, dq2, dk0, dk1, dk2, dv0, dv1, dv2, gate_out, carry_ref, stage_ref):
    t = pl.program_id(1)
    lo = _lane_lo()
    gq_fox, gk_fox, gq_dil, gk_dil = (gains_ref[i:i + 1, :] for i in range(4))

    x = x_ref[...]
    ms = jnp.mean(x * x, axis=-1, keepdims=True)
    h = (x * lax.rsqrt(ms + EPS) * g_ref[...]).astype(BF16)

    def proj(col, width):
        return _dot(h, w_ref[:, col:col + width])

    def gate_chunk(i):
        cols = slice(i * PROJ_CHUNK, (i + 1) * PROJ_CHUNK)
        gate_out[:, cols] = proj(OFF_GATES + i * PROJ_CHUNK, PROJ_CHUNK).astype(BF16)

    @pl.when(t == 0)
    def _():
        carry_ref[...] = jnp.zeros_like(carry_ref)

    z = proj(OFF_FORGET, LANES) + bf_ref[...]
    log_f = jnp.minimum(z, 0.0) - jnp.log(1.0 + jnp.exp(-jnp.abs(z)))
    tri = tri_ref[...]
    hi, mid, lw = _split3(log_f)
    f_cum = (_dot(tri, hi) + _dot(tri, mid) + _dot(tri, lw)) + carry_ref[0:1, :]
    carry_ref[...] = jnp.broadcast_to(f_cum[TM - 1:TM, :], carry_ref.shape)
    hi, mid, lw = _split3(f_cum * LOG2E)
    aug = _dot(jnp.concatenate([hi, mid, lw], axis=1), eaug_ref[...]) + ones_ref[...]

    hr = lax.broadcasted_iota(jnp.int32, (MXU_COLS, MXU_COLS), 0) // HEAD_DIM
    hc = lax.broadcasted_iota(jnp.int32, (MXU_COLS, MXU_COLS), 1) // HEAD_DIM
    head_sum = jnp.where(hr == hc, 1.0, 0.0).astype(BF16)

    fox_q = _head_rms(proj(OFF_FOX, W_FOX), head_sum)
    fox_k = _head_rms(proj(OFF_FOX + W_FOX, W_FOX), head_sum)
    for p in range(W_FOX // PAIR):
        sl = slice(p * PAIR, (p + 1) * PAIR)
        qn = fox_q[:, sl] * gq_fox
        kn = fox_k[:, sl] * gk_fox
        for hh in range(2):
            hd = 2 * p + hh
            own = lo if hh == 0 else jnp.logical_not(lo)
            _put_transposed(fqt_out, hd, jnp.where(own, qn, aug[:, hd * LANES:(hd + 1) * LANES]), TQ)
            ka = aug[:, (N_HEADS_FOX + hd) * LANES:(N_HEADS_FOX + hd + 1) * LANES]
            fk_out[:, hd * LANES:(hd + 1) * LANES] = jnp.where(own, kn, ka).astype(BF16)
    gate_chunk(0)
    for out, col, width, mult in ((fvt_out, OFF_FOX + 2 * W_FOX, TK, None), (sqt_out, OFF_SB, TQ, SCALE * LOG2E),
                                  (svt_out, OFF_SB + 2 * W_SB, TK, None)):
        y = proj(col, W_SB)
        for p in range(W_SB // PAIR):
            tile = y[:, p * PAIR:(p + 1) * PAIR]
            _put_transposed(out, p, tile if mult is None else tile * mult, width)
    sbk_out[...] = proj(OFF_SB + W_SB, W_SB).astype(BF16)
    gate_chunk(1)

    ang_t = pos_ref[...].astype(F32) * invf_ref[...]
    reps = LANES // (HEAD_DIM // 2)
    cos = jnp.transpose(jnp.concatenate([jnp.cos(ang_t)] * reps, axis=0))
    sin_signed = jnp.transpose(jnp.concatenate([jnp.sin(ang_t)] * reps, axis=0)) * rope_ref[0:1, :]
    first_half = rope_ref[1:2, :] > 0.5

    def rope(x):
        partner = jnp.where(first_half, pltpu.roll(x, LANES - HEAD_DIM // 2, 1), pltpu.roll(x, HEAD_DIM // 2, 1))
        return x * cos + partner * sin_signed

    npair_dil = W_DIL // PAIR

    def put_streams(outs):
        for g, (_, dil) in enumerate(DIL_PATTERNS):
            for s in range(dil):
                rows = pl.ds(s, TM // dil, stride=dil) if dil > 1 else slice(None)
                for pp in range(npair_dil):
                    col = s * W_DIL + pp * PAIR
                    outs[g][:, col:col + PAIR] = stage_ref[g * npair_dil + pp, rows, :].astype(BF16)

    for i, (gain, outs) in enumerate(((gq_dil, (dq0, dq1, dq2)), (gk_dil, (dk0, dk1, dk2)), (None, (dv0, dv1, dv2)))):
        y = proj(OFF_DIL + i * W_DIL_ALL, W_DIL_ALL)
        if gain is not None:
            y = _head_rms(y, head_sum)
        for p in range(W_DIL_ALL // PAIR):
            tile = y[:, p * PAIR:(p + 1) * PAIR]
            stage_ref[p] = tile if gain is None else rope(tile * gain)
        put_streams(outs)
        if 2 + i < SZ_GATES // PROJ_CHUNK:
            gate_chunk(2 + i)


def _aug_tables():
    e = np.zeros((3, LANES, 2 * N_HEADS_FOX * LANES), np.float32)
    ones = np.zeros((1, 2 * N_HEADS_FOX * LANES), np.float32)
    for h in range(N_HEADS_FOX):
        other = HEAD_DIM if h % 2 == 0 else 0
        qbase, kbase = h * LANES + other, (N_HEADS_FOX + h) * LANES + other
        for part in range(3):
            e[part, h, qbase + AUG_F + part] = 1.0
            ones[0, kbase + AUG_F + part] = 1.0
            e[part, h, kbase + AUG_ONE + part] = -1.0
            ones[0, qbase + AUG_ONE + part] = 1.0
    return jnp.asarray(e.reshape(3 * LANES, -1), BF16), jnp.asarray(ones, F32)


def _front(x2, g, w, pos, gains, bf, tables, batch, seq):
    n = x2.shape[0]
    nt = seq // TM
    row = lambda b, t: (b * nt + t, 0)
    tchunk = lambda heads, w: pl.BlockSpec((None, heads, TM // w, LANES, w), lambda b, t: (b, 0, t, 0, 0))
    tshape = lambda heads, w: jax.ShapeDtypeStruct((batch, heads, seq // w, LANES, w), BF16)
    stream_specs = [pl.BlockSpec((None, TM // d, d * W_DIL), lambda b, t: (b, t, 0)) for _, d in DIL_PATTERNS] * 3
    stream_shapes = [jax.ShapeDtypeStruct((batch, seq // d, d * W_DIL), BF16) for _, d in DIL_PATTERNS] * 3
    npair = W_FOX // PAIR
    return pl.pallas_call(
        _front_kernel,
        grid=(batch, nt),
        in_specs=[pl.BlockSpec((TM, D_MODEL), row), _resident((1, D_MODEL)), _resident(w.shape),
                  pl.BlockSpec((None, 1, TM), lambda b, t: (b * nt + t, 0, 0)),
                  _resident((4, LANES)), _resident((1, LANES))]
                 + [_resident(a.shape) for a in tables],
        out_specs=[tchunk(N_HEADS_FOX, TQ), pl.BlockSpec((TM, N_HEADS_FOX * LANES), row), tchunk(npair, TK),
                   tchunk(npair, TQ), pl.BlockSpec((TM, W_SB), row), tchunk(npair, TK)] + stream_specs
                  + [pl.BlockSpec((TM, SZ_GATES), row)],
        out_shape=[tshape(N_HEADS_FOX, TQ), jax.ShapeDtypeStruct((n, N_HEADS_FOX * LANES), BF16), tshape(npair, TK),
                   tshape(npair, TQ), jax.ShapeDtypeStruct((n, W_SB), BF16), tshape(npair, TK)] + stream_shapes
                  + [jax.ShapeDtypeStruct((n, SZ_GATES), BF16)],
        scratch_shapes=[pltpu.VMEM((8, LANES), F32), pltpu.VMEM((W_DIL_ALL // PAIR, TM, LANES), F32)],
        compiler_params=_params(("arbitrary", "arbitrary")),
        name="front",
    )(x2, g, w, pos, gains, bf, *tables)


def _chunk_rows(j):
    return pl.ds(j * TK if isinstance(j, int) else pl.multiple_of(j * TK, TK), TK)


def _attn_scratch():
    return [pltpu.VMEM((2, TK, TQ), F32), pltpu.VMEM((2, TK, TQ), F32), pltpu.VMEM((2, HEAD_DIM, TQ), F32)]


def _fox_kernel(qt_ref, k_ref, vt_ref, o_ref, sa_ref, sb_ref, acc_ref, *, seq):
    assert TQ == 2 * TK
    r = lax.broadcasted_iota(jnp.int32, (TK, TQ), 0)
    c = lax.broadcasted_iota(jnp.int32, (TK, TQ), 1)
    causal = r <= c
    causal_late = (lax.broadcasted_iota(jnp.int32, (TK, TK), 0)
                   <= lax.broadcasted_iota(jnp.int32, (TK, TK), 1))
    late = slice(TK, TQ)

    for qi in range(seq // TQ):
        qt = (qt_ref[0, qi], qt_ref[1, qi])

        def produce(j, buf, cols=slice(None)):
            for hh in range(2):
                buf[hh, :, cols] = _dot(k_ref[_chunk_rows(j), hh * LANES:(hh + 1) * LANES], qt[hh][:, cols])

        def consume(j, buf, state, mask, cols=slice(None)):
            vt = vt_ref[j]
            out = []
            for hh in range(2):
                m_all, l_all = state[2 * hh:2 * hh + 2]
                m_prev, l_prev = m_all[:, cols], l_all[:, cols]
                s = buf[hh, :, cols]
                if mask is not None:
                    s = jnp.where(mask, s, NEG)
                m_new = jnp.maximum(m_prev, jnp.max(s, axis=0, keepdims=True))
                alpha = jnp.exp2(m_prev - m_new)
                p = jnp.exp2(s - m_new)
                l_new = alpha * l_prev + jnp.sum(p, axis=0, keepdims=True)
                pv = _dot(vt[hh * HEAD_DIM:(hh + 1) * HEAD_DIM, :], p.astype(BF16))
                acc_ref[hh, :, cols] = alpha * acc_ref[hh, :, cols] + pv
                if cols != slice(None):
                    m_new = jnp.concatenate([m_all[:, :cols.start], m_new], axis=1)
                    l_new = jnp.concatenate([l_all[:, :cols.start], l_new], axis=1)
                out += [m_new, l_new]
            return tuple(out)

        n_full = qi * (TQ // TK)
        acc_ref[...] = jnp.zeros_like(acc_ref)
        init = (jnp.full((1, TQ), NEG, F32), jnp.zeros((1, TQ), F32))
        st = init + init
        produce(0, sa_ref)
        if n_full > 0:
            def body(i, st):
                j = 2 * i
                produce(j + 1, sb_ref)
                st = consume(j, sa_ref, st, None)
                produce(j + 2, sa_ref)
                return consume(j + 1, sb_ref, st, None)
            st = lax.fori_loop(0, n_full // 2, body, st)
        produce(n_full + 1, sb_ref, late)
        st = consume(n_full, sa_ref, st, causal)
        st = consume(n_full + 1, sb_ref, st, causal_late, late)
        out_t = jnp.concatenate([acc_ref[0] * (1.0 / st[1]), acc_ref[1] * (1.0 / st[3])], axis=0)
        o_ref[qi * TQ:(qi + 1) * TQ, :] = jnp.transpose(out_t).astype(BF16)


def _fox_attention(fqt, fk, fvt, batch, seq):
    n = fk.shape[0]
    npair = W_FOX // PAIR
    return pl.pallas_call(
        functools.partial(_fox_kernel, seq=seq),
        grid=(batch, npair),
        in_specs=[pl.BlockSpec((None, 2, seq // TQ, LANES, TQ), lambda b, p: (b, p, 0, 0, 0)),
                  pl.BlockSpec((seq, 2 * LANES), lambda b, p: (b, p)),
                  pl.BlockSpec((None, None, seq // TK, LANES, TK), lambda b, p: (b, p, 0, 0, 0))],
        out_specs=pl.BlockSpec((seq, PAIR), lambda b, p: (b, p)),
        out_shape=jax.ShapeDtypeStruct((n, W_FOX), BF16),
        scratch_shapes=_attn_scratch(),
        compiler_params=_params(("arbitrary", "arbitrary")),
        name="fox_attn",
    )(fqt, fk, fvt)


def _sb_kernel(qt_ref, k_ref, vt_ref, o_ref, za_ref, zb_ref, acc_ref, *, seq):
    assert TQ == 2 * TK
    lo = _lane_lo()
    r = lax.broadcasted_iota(jnp.int32, (TK, TQ), 0)
    c = lax.broadcasted_iota(jnp.int32, (TK, TQ), 1)
    strict = r < c
    rk = lax.broadcasted_iota(jnp.int32, (TK, TK), 0)
    ck = lax.broadcasted_iota(jnp.int32, (TK, TK), 1)
    strict_late = rk < ck
    late = slice(TK, TQ)
    tri = jnp.where(rk <= ck, 1.0, 0.0).astype(BF16)
    zero = jnp.zeros((), BF16)

    for qi in range(seq // TQ):
        qt = qt_ref[qi]

        def produce(j, buf, cols=slice(None)):
            kc = k_ref[_chunk_rows(j), :]
            buf[0, :, cols] = _dot(jnp.where(lo, kc, zero), qt[:, cols])
            buf[1, :, cols] = _dot(jnp.where(lo, zero, kc), qt[:, cols])

        def consume(j, buf, runs, mask, cols=slice(None)):
            vt = vt_ref[j]
            suffix = []
            for hh in range(2):
                z = buf[hh, :, cols]
                sp = jnp.maximum(z, 0.0) + jnp.log(1.0 + jnp.exp2(-jnp.abs(z))) * LOG2E
                if mask is not None:
                    sp = jnp.where(mask, sp, 0.0)
                suffix.append(_dot(tri, sp.astype(BF16)))
            out = []
            for hh in range(2):
                a = jnp.exp2(buf[hh, :, cols] - suffix[hh] - runs[hh][:, cols])
                if mask is not None:
                    a = jnp.where(mask, a, 0.0)
                acc_ref[hh, :, cols] += _dot(vt[hh * HEAD_DIM:(hh + 1) * HEAD_DIM, :], a.astype(BF16))
                run = runs[hh][:, cols] + suffix[hh][0:1, :]
                if cols != slice(None):
                    run = jnp.concatenate([runs[hh][:, :cols.start], run], axis=1)
                out.append(run)
            return tuple(out)

        n_full = qi * (TQ // TK)
        acc_ref[...] = jnp.zeros_like(acc_ref)
        runs = (jnp.zeros((1, TQ), F32),) * 2
        produce(n_full + 1, zb_ref, late)
        produce(n_full, za_ref)
        runs = consume(n_full + 1, zb_ref, runs, strict_late, late)
        if n_full > 0:
            produce(n_full - 1, zb_ref)
        runs = consume(n_full, za_ref, runs, strict)
        if n_full > 0:
            def body(i, runs):
                j = n_full - 1 - 2 * i
                produce(j - 1, za_ref)
                runs = consume(j, zb_ref, runs, None)
                produce(j - 2, zb_ref)
                return consume(j - 1, za_ref, runs, None)
            if n_full > 2:
                runs = lax.fori_loop(0, n_full // 2 - 1, body, runs)
            produce(0, za_ref)
            runs = consume(1, zb_ref, runs, None)
            runs = consume(0, za_ref, runs, None)
        out_t = jnp.concatenate([acc_ref[0], acc_ref[1]], axis=0)
        o_ref[qi * TQ:(qi + 1) * TQ, :] = jnp.transpose(out_t).astype(BF16)


def _sb_attention(sqt, sbk, svt, batch, seq):
    n = sbk.shape[0]
    npair = W_SB // PAIR
    tspec = lambda w: pl.BlockSpec((None, None, seq // w, LANES, w), lambda b, p: (b, p, 0, 0, 0))
    return pl.pallas_call(
        functools.partial(_sb_kernel, seq=seq),
        grid=(batch, npair),
        in_specs=[tspec(TQ), pl.BlockSpec((seq, PAIR), lambda b, p: (b, p)), tspec(TK)],
        out_specs=pl.BlockSpec((seq, PAIR), lambda b, p: (b, p)),
        out_shape=jax.ShapeDtypeStruct((n, W_SB), BF16),
        scratch_shapes=_attn_scratch(),
        compiler_params=_params(("arbitrary", "arbitrary")),
        name="sb_attn",
    )(sqt, sbk, svt)


def _dil_kernel(q_ref, k_ref, v_ref, o_ref, lse_ref, sa_ref, sb_ref, *, n, dil):
    lo = _lane_lo()
    keeps = (lo, jnp.logical_not(lo))
    r = lax.broadcasted_iota(jnp.int32, (DQ, 2 * DQ), 0)
    c = lax.broadcasted_iota(jnp.int32, (DQ, 2 * DQ), 1)
    band = (c >= r) & (c <= r + DQ)
    causal = (lax.broadcasted_iota(jnp.int32, (DQ, DQ), 1)
              <= lax.broadcasted_iota(jnp.int32, (DQ, DQ), 0))
    zero = jnp.zeros((), BF16)
    npair = W_DIL // PAIR

    def key_rows(blk):
        return slice(0, DQ) if blk == 0 else slice((blk - 1) * DQ, (blk + 1) * DQ)

    def produce(unit, buf):
        stream, blk = unit
        rows, krows = slice(blk * DQ, (blk + 1) * DQ), key_rows(blk)
        for pp in range(npair):
            cols = slice(stream * W_DIL + pp * PAIR, stream * W_DIL + (pp + 1) * PAIR)
            q, k = q_ref[rows, cols], k_ref[krows, cols]
            for hh, keep in enumerate(keeps):
                buf[2 * pp + hh, :, 0:krows.stop - krows.start] = _dot_nt(jnp.where(keep, q, zero), k)

    def consume(unit, buf):
        stream, blk = unit
        rows, krows = slice(blk * DQ, (blk + 1) * DQ), key_rows(blk)
        nk = krows.stop - krows.start
        mask = causal if blk == 0 else band
        for pp in range(npair):
            cols = slice(stream * W_DIL + pp * PAIR, stream * W_DIL + (pp + 1) * PAIR)
            v = v_ref[krows, cols]
            outs, lses = [], []
            for hh, keep in enumerate(keeps):
                s = jnp.where(mask, buf[2 * pp + hh, :, 0:nk], NEG)
                m = jnp.max(s, axis=-1, keepdims=True)
                p = jnp.exp(s - m)
                den = jnp.sum(p, axis=-1, keepdims=True)
                o = _dot(p.astype(BF16), jnp.where(keep, v, zero))
                outs.append(o * (1.0 / den))
                lses.append(m + jnp.log(den))
            o_ref[rows, cols] = (outs[0] + outs[1]).astype(BF16)
            lse_ref[rows, cols] = jnp.where(lo, lses[0], lses[1])

    units = [(stream, blk) for stream in range(dil) for blk in range(n // DQ)]
    bufs = (sa_ref, sb_ref)
    produce(units[0], bufs[0])
    for u, unit in enumerate(units):
        if u + 1 < len(units):
            produce(units[u + 1], bufs[(u + 1) % 2])
        consume(unit, bufs[u % 2])


def _dil_attention(dq, dk, dv, group, batch, seq):
    window, dil = DIL_PATTERNS[group]
    assert window // dil == DQ
    n = seq // dil
    spec = pl.BlockSpec((None, n, dil * W_DIL), lambda b: (b, 0, 0))
    return pl.pallas_call(
        functools.partial(_dil_kernel, n=n, dil=dil),
        grid=(batch,),
        in_specs=[spec, spec, spec],
        out_specs=[spec, spec],
        out_shape=[jax.ShapeDtypeStruct((batch, n, dil * W_DIL), BF16),
                   jax.ShapeDtypeStruct((batch, n, dil * W_DIL), F32)],
        scratch_shapes=[pltpu.VMEM((N_HEADS_DIL, DQ, 2 * DQ), F32)] * 2,
        compiler_params=_params(("arbitrary",)),
        name=f"dil_attn_g{group}",
    )(dq, dk, dv)


def _merge_kernel(x_ref, oa_ref, ob_ref, o0_ref, o1_ref, o2_ref, l0_ref, l1_ref, l2_ref, gate_ref,
                  wa_ref, wb_ref, wc_ref, wo_ref, out_ref, stage_ref):
    def tokens(ref, g):
        dil = DIL_PATTERNS[g][1]
        if dil == 1:
            return ref[...].astype(F32)
        npair = W_DIL // PAIR
        for s in range(dil):
            for pp in range(npair):
                col = s * W_DIL + pp * PAIR
                stage_ref[g * npair + pp, pl.ds(s, TM // dil, stride=dil), :] = ref[:, col:col + PAIR].astype(F32)
        return jnp.concatenate([stage_ref[g * npair + pp] for pp in range(npair)], axis=1)

    lses = [tokens(ref, g) for g, ref in enumerate((l0_ref, l1_ref, l2_ref))]
    mx = jnp.maximum(jnp.maximum(lses[0], lses[1]), lses[2])
    es = [jnp.exp(l - mx) for l in lses]
    inv = 1.0 / (es[0] + es[1] + es[2])
    out_c = None
    for g, ref in enumerate((o0_ref, o1_ref, o2_ref)):
        term = (es[g] * inv) * tokens(ref, g)
        out_c = term if out_c is None else out_c + term
    ys = (_dot(oa_ref[...], wa_ref[...]), _dot(ob_ref[...], wb_ref[...]), _dot(out_c.astype(BF16), wc_ref[...]))
    merged = None
    for b in range(N_BRANCHES):
        gate = 1.0 / (1.0 + jnp.exp(-gate_ref[:, b * D_MODEL:(b + 1) * D_MODEL].astype(F32)))
        merged = gate * ys[b] if merged is None else merged + gate * ys[b]
    out_ref[...] = x_ref[...] + _dot(merged.astype(BF16), wo_ref[...])


def _merge(x2, out_a, out_b, os_, lses, gates, wa, wb, wc, wo, seq):
    n = x2.shape[0]
    nt = seq // TM
    row = lambda i: (i, 0)
    tile = lambda w: pl.BlockSpec((TM, w), row)
    streams = [pl.BlockSpec((None, TM // d, d * W_DIL), lambda i: (i // nt, i % nt, 0)) for _, d in DIL_PATTERNS]
    return pl.pallas_call(
        _merge_kernel,
        grid=(n // TM,),
        in_specs=[tile(D_MODEL), tile(W_FOX), tile(W_SB)] + streams * 2 + [tile(SZ_GATES)]
                 + [_resident(w.shape) for w in (wa, wb, wc, wo)],
        out_specs=tile(D_MODEL),
        out_shape=jax.ShapeDtypeStruct((n, D_MODEL), F32),
        scratch_shapes=[pltpu.VMEM((W_DIL_ALL // PAIR, TM, LANES), F32)],
        compiler_params=_params(("arbitrary",)),
        name="merge",
    )(x2, out_a, out_b, *os_, *lses, gates, wa, wb, wc, wo)


def _mlp_kernel(x_ref, g_ref, w1_ref, w2_ref, out_ref):
    x = x_ref[...]
    ms = jnp.mean(x * x, axis=-1, keepdims=True)
    h = (x * lax.rsqrt(ms + EPS) * g_ref[...]).astype(BF16)
    acc = x
    for c in range(0, D_FF, FF_CHUNK):
        u = jnp.maximum(_dot(h, w1_ref[:, c:c + FF_CHUNK]), 0.0)
        acc = acc + _dot((u * u).astype(BF16), w2_ref[c:c + FF_CHUNK, :])
    out_ref[...] = acc


def _mlp(x2, g, w1, w2):
    n = x2.shape[0]
    row = lambda i: (i, 0)
    return pl.pallas_call(
        _mlp_kernel,
        grid=(n // TM,),
        in_specs=[pl.BlockSpec((TM, D_MODEL), row), _resident((1, D_MODEL)),
                  _resident(w1.shape), _resident(w2.shape)],
        out_specs=pl.BlockSpec((TM, D_MODEL), row),
        out_shape=jax.ShapeDtypeStruct((n, D_MODEL), F32),
        compiler_params=_params(("arbitrary",)),
        name="mlp",
    )(x2, g, w1, w2)


def _prep_tables():
    half = HEAD_DIM // 2
    lane = np.arange(LANES)
    inv = (1.0 / (ROPE_THETA ** (np.arange(half, dtype=np.float32) / half))).astype(np.float32)
    first = (lane % HEAD_DIM) < half
    rope_tab = jnp.asarray(np.stack([np.where(first, -1.0, 1.0), first.astype(np.float32)]), F32)
    invf = jnp.asarray(np.repeat(inv[:, None], TM, axis=1), F32)
    tri = jnp.asarray(np.tril(np.ones((TM, TM), np.float32)), BF16)
    return (rope_tab, invf, tri) + _aug_tables()


def kernel(x, positions, attn_norm, w_in, b_forget, q_norm_fox, k_norm_fox, q_norm_dil, k_norm_dil,
           w_up_fox, w_up_sb, w_up_dil, w_out, mlp_norm, w_mlp_in, w_mlp_out):
    batch, seq, _ = x.shape
    depth = w_in.shape[0]
    n = batch * seq
    o1 = SZ_FOX_QKV
    o2 = o1 + SZ_FORGET
    w_forget = jnp.pad(w_in[:, :, o1:o2], ((0, 0), (0, 0), (0, LANES - SZ_FORGET)))
    w_all = jnp.concatenate([w_in[:, :, :o1], w_in[:, :, o2:], w_forget], axis=2).astype(BF16)
    wa, wb, wc, wo = (w.astype(BF16) for w in (w_up_fox, w_up_sb, w_up_dil, w_out))
    w1, w2 = w_mlp_in.astype(BF16), w_mlp_out.astype(BF16)
    pair = lambda g: jnp.tile(g.astype(F32), (1, LANES // HEAD_DIM))
    gains = jnp.stack([pair(q_norm_fox) * (SCALE * LOG2E), pair(k_norm_fox),
                       pair(q_norm_dil) * SCALE, pair(k_norm_dil)], axis=1)
    bf = jnp.pad(b_forget.astype(F32), ((0, 0), (0, LANES - SZ_FORGET)))[:, None, :]
    tables = _prep_tables()
    pos = positions.reshape(n // TM, 1, TM)

    x2 = x.reshape(n, D_MODEL)
    for l in range(depth):
        front = _front(x2, attn_norm[l][None, :], w_all[l], pos, gains[l], bf[l], tables, batch, seq)
        fqt, fk, fvt, sqt, sbk, svt = front[:6]
        dqs, dks, dvs, gates = front[6:9], front[9:12], front[12:15], front[15]
        out_a = _fox_attention(fqt, fk, fvt, batch, seq)
        out_b = _sb_attention(sqt, sbk, svt, batch, seq)
        dil = [_dil_attention(dqs[g], dks[g], dvs[g], g, batch, seq) for g in range(N_DIL_GROUPS)]
        x2 = _merge(x2, out_a, out_b, [d[0] for d in dil], [d[1] for d in dil], gates,
                    wa[l], wb[l], wc[l], wo[l], seq)
        x2 = _mlp(x2, mlp_norm[l][None, :], w1[l], w2[l])
    return x2.reshape(batch, seq, D_MODEL)
```

```python
import functools
import math

import numpy as np
import jax
import jax.numpy as jnp
from jax import lax
from jax.experimental import pallas as pl
from jax.experimental.pallas import tpu as pltpu

D_MODEL = 1024
HEAD_DIM = 64
N_HEADS_FOX = 8
N_HEADS_SB = 8
DIL_PATTERNS = ((128, 1), (512, 4), (2048, 16))
N_DIL_GROUPS = len(DIL_PATTERNS)
N_HEADS_DIL = 4
D_FF = 4 * D_MODEL
ROPE_THETA = 10000.0
EPS = 1e-6
N_BRANCHES = 3

W_FOX = N_HEADS_FOX * HEAD_DIM
W_SB = N_HEADS_SB * HEAD_DIM
W_DIL = N_HEADS_DIL * HEAD_DIM
W_DIL_ALL = N_DIL_GROUPS * W_DIL
SZ_FOX_QKV = 3 * W_FOX
SZ_FORGET = N_HEADS_FOX
SZ_SB_QKV = 3 * W_SB
SZ_DIL_QKV = 3 * W_DIL_ALL
SZ_GATES = N_BRANCHES * D_MODEL

LANES = 128
PAIR = LANES
SCALE = 1.0 / math.sqrt(HEAD_DIM)
LOG2E = math.log2(math.e)
NEG = -1e30

TM = 512
TQ = 512
TK = 256
FOX_HEADS_PER_STEP = 4
DQ = 128
PROJ_CHUNK = 768
FF_CHUNK = 1024
VMEM_LIMIT = 56 * 1024 * 1024

AUG_F = 0
AUG_ONE = 3

F32 = jnp.float32
BF16 = jnp.bfloat16


def _params(sem):
    return pltpu.CompilerParams(dimension_semantics=sem, vmem_limit_bytes=VMEM_LIMIT)


def _resident(shape):
    nd = len(shape)
    return pl.BlockSpec(shape, lambda *_: (0,) * nd, pipeline_mode=pl.Buffered(1))


def _dot(a, b):
    return jnp.dot(a, b, preferred_element_type=F32)


def _dot_nt(a, b):
    return lax.dot_general(a, b, (((1,), (1,)), ((), ())), preferred_element_type=F32)


def _split3(x):
    hi = x.astype(BF16)
    r1 = x - hi.astype(F32)
    mid = r1.astype(BF16)
    lo = (r1 - mid.astype(F32)).astype(BF16)
    return hi, mid, lo


def _lane_lo():
    return lax.broadcasted_iota(jnp.int32, (1, LANES), 1) < HEAD_DIM


OFF_FOX = 0
OFF_SB = OFF_FOX + SZ_FOX_QKV
OFF_DIL = OFF_SB + SZ_SB_QKV
OFF_GATES = OFF_DIL + SZ_DIL_QKV
OFF_FORGET = OFF_GATES + SZ_GATES
MXU_COLS = 256


def _head_rms(y, head_sum):
    sq = (y * y).astype(BF16)
    ss = jnp.concatenate([_dot(sq[:, c:c + MXU_COLS], head_sum) for c in range(0, y.shape[1], MXU_COLS)], axis=1)
    return y * lax.rsqrt(ss * (1.0 / HEAD_DIM) + EPS)


def _put_transposed(out_ref, idx, tile, width):
    tt = jnp.transpose(tile).astype(BF16)
    for i in range(TM // width):
        out_ref[idx, i] = tt[:, i * width:(i + 1) * width]


def _front_kernel(x_ref, g_ref, w_ref, pos_ref, gains_ref, bf_ref, rope_ref, invf_ref, tri_ref, eaug_ref, ones_ref,
                  fqt_out, fk_out, fvt_out, sqt_out, sbk_out, svt_out,
                  dq0, dq1, dq2, dk0, dk1, dk2, dv0, dv1, dv2, gate_out, carry_ref, stage_ref):
    t = pl.program_id(1)
    lo = _lane_lo()
    gq_fox, gk_fox, gq_dil, gk_dil = (gains_ref[i:i + 1, :] for i in range(4))

    x = x_ref[...]
    ms = jnp.mean(x * x, axis=-1, keepdims=True)
    h = (x * lax.rsqrt(ms + EPS) * g_ref[...]).astype(BF16)

    def proj(col, width):
        return _dot(h, w_ref[:, col:col + width])

    gate_cols = iter(range(0, SZ_GATES, MXU_COLS))

    def gate_step():
        c = next(gate_cols, None)
        if c is not None:
            gate_out[:, c:c + MXU_COLS] = proj(OFF_GATES + c, MXU_COLS).astype(BF16)
        return c is not None

    @pl.when(t == 0)
    def _():
        carry_ref[...] = jnp.zeros_like(carry_ref)

    z = proj(OFF_FORGET, LANES) + bf_ref[...]
    log_f = jnp.minimum(z, 0.0) - jnp.log(1.0 + jnp.exp(-jnp.abs(z)))
    tri = tri_ref[...]
    hi, mid, lw = _split3(log_f)
    f_cum = (_dot(tri, hi) + _dot(tri, mid) + _dot(tri, lw)) + carry_ref[0:1, :]
    carry_ref[...] = jnp.broadcast_to(f_cum[TM - 1:TM, :], carry_ref.shape)
    hi, mid, lw = _split3(f_cum * LOG2E)
    aug = _dot(jnp.concatenate([hi, mid, lw], axis=1), eaug_ref[...]) + ones_ref[...]

    hr = lax.broadcasted_iota(jnp.int32, (MXU_COLS, MXU_COLS), 0) // HEAD_DIM
    hc = lax.broadcasted_iota(jnp.int32, (MXU_COLS, MXU_COLS), 1) // HEAD_DIM
    head_sum = jnp.where(hr == hc, 1.0, 0.0).astype(BF16)

    fox_q = _head_rms(proj(OFF_FOX, W_FOX), head_sum)
    fox_k = _head_rms(proj(OFF_FOX + W_FOX, W_FOX), head_sum)
    for p in range(W_FOX // PAIR):
        sl = slice(p * PAIR, (p + 1) * PAIR)
        qn = fox_q[:, sl] * gq_fox
        kn = fox_k[:, sl] * gk_fox
        for hh in range(2):
            hd = 2 * p + hh
            own = lo if hh == 0 else jnp.logical_not(lo)
            _put_transposed(fqt_out, hd, jnp.where(own, qn, aug[:, hd * LANES:(hd + 1) * LANES]), TQ)
            ka = aug[:, (N_HEADS_FOX + hd) * LANES:(N_HEADS_FOX + hd + 1) * LANES]
            fk_out[:, hd * LANES:(hd + 1) * LANES] = jnp.where(own, kn, ka).astype(BF16)
        gate_step()
    for out, col, width, mult in ((fvt_out, OFF_FOX + 2 * W_FOX, TK, None), (sqt_out, OFF_SB, TQ, SCALE),
                                  (svt_out, OFF_SB + 2 * W_SB, TK, None)):
        y = proj(col, W_SB)
        for p in range(W_SB // PAIR):
            tile = y[:, p * PAIR:(p + 1) * PAIR]
            _put_transposed(out, p, tile if mult is None else tile * mult, width)
        gate_step()
    sbk_out[...] = proj(OFF_SB + W_SB, W_SB).astype(BF16)

    ang_t = pos_ref[...].astype(F32) * invf_ref[...]
    reps = LANES // (HEAD_DIM // 2)
    cos = jnp.transpose(jnp.concatenate([jnp.cos(ang_t)] * reps, axis=0))
    sin_signed = jnp.transpose(jnp.concatenate([jnp.sin(ang_t)] * reps, axis=0)) * rope_ref[0:1, :]
    first_half = rope_ref[1:2, :] > 0.5

    def rope(x):
        partner = jnp.where(first_half, pltpu.roll(x, LANES - HEAD_DIM // 2, 1), pltpu.roll(x, HEAD_DIM // 2, 1))
        return x * cos + partner * sin_signed

    npair_dil = W_DIL // PAIR

    def put_streams(outs):
        for g, (_, dil) in enumerate(DIL_PATTERNS):
            for s in range(dil):
                rows = pl.ds(s, TM // dil, stride=dil) if dil > 1 else slice(None)
                for pp in range(npair_dil):
                    col = s * W_DIL + pp * PAIR
                    outs[g][:, col:col + PAIR] = stage_ref[g * npair_dil + pp, rows, :].astype(BF16)

    for i, (gain, outs) in enumerate(((gq_dil, (dq0, dq1, dq2)), (gk_dil, (dk0, dk1, dk2)), (None, (dv0, dv1, dv2)))):
        y = proj(OFF_DIL + i * W_DIL_ALL, W_DIL_ALL)
        if gain is not None:
            y = _head_rms(y, head_sum)
        for p in range(W_DIL_ALL // PAIR):
            tile = y[:, p * PAIR:(p + 1) * PAIR]
            stage_ref[p] = tile if gain is None else rope(tile * gain)
            if p % 3 == 2:
                gate_step()
        put_streams(outs)
    while gate_step():
        pass


def _aug_tables():
    e = np.zeros((3, LANES, 2 * N_HEADS_FOX * LANES), np.float32)
    ones = np.zeros((1, 2 * N_HEADS_FOX * LANES), np.float32)
    for h in range(N_HEADS_FOX):
        other = HEAD_DIM if h % 2 == 0 else 0
        qbase, kbase = h * LANES + other, (N_HEADS_FOX + h) * LANES + other
        for part in range(3):
            e[part, h, qbase + AUG_F + part] = 1.0
            ones[0, kbase + AUG_F + part] = 1.0
            e[part, h, kbase + AUG_ONE + part] = -1.0
            ones[0, qbase + AUG_ONE + part] = 1.0
    return jnp.asarray(e.reshape(3 * LANES, -1), BF16), jnp.asarray(ones, F32)


def _front(x2, g, w, pos, gains, bf, tables, batch, seq):
    n = x2.shape[0]
    nt = seq // TM
    row = lambda b, t: (b * nt + t, 0)
    tchunk = lambda heads, w: pl.BlockSpec((None, heads, TM // w, LANES, w), lambda b, t: (b, 0, t, 0, 0))
    tshape = lambda heads, w: jax.ShapeDtypeStruct((batch, heads, seq // w, LANES, w), BF16)
    stream_specs = [pl.BlockSpec((None, TM // d, d * W_DIL), lambda b, t: (b, t, 0)) for _, d in DIL_PATTERNS] * 3
    stream_shapes = [jax.ShapeDtypeStruct((batch, seq // d, d * W_DIL), BF16) for _, d in DIL_PATTERNS] * 3
    npair = W_FOX // PAIR
    return pl.pallas_call(
        _front_kernel,
        grid=(batch, nt),
        in_specs=[pl.BlockSpec((TM, D_MODEL), row), _resident((1, D_MODEL)), _resident(w.shape),
                  pl.BlockSpec((None, 1, TM), lambda b, t: (b * nt + t, 0, 0)),
                  _resident((4, LANES)), _resident((1, LANES))]
                 + [_resident(a.shape) for a in tables],
        out_specs=[tchunk(N_HEADS_FOX, TQ), pl.BlockSpec((TM, N_HEADS_FOX * LANES), row), tchunk(npair, TK),
                   tchunk(npair, TQ), pl.BlockSpec((TM, W_SB), row), tchunk(npair, TK)] + stream_specs
                  + [pl.BlockSpec((TM, SZ_GATES), row)],
        out_shape=[tshape(N_HEADS_FOX, TQ), jax.ShapeDtypeStruct((n, N_HEADS_FOX * LANES), BF16), tshape(npair, TK),
                   tshape(npair, TQ), jax.ShapeDtypeStruct((n, W_SB), BF16), tshape(npair, TK)] + stream_shapes
                  + [jax.ShapeDtypeStruct((n, SZ_GATES), BF16)],
        scratch_shapes=[pltpu.VMEM((8, LANES), F32), pltpu.VMEM((W_DIL_ALL // PAIR, TM, LANES), F32)],
        compiler_params=_params(("arbitrary", "arbitrary")),
        name="front",
    )(x2, g, w, pos, gains, bf, *tables)


def _chunk_rows(j):
    return pl.ds(j * TK if isinstance(j, int) else pl.multiple_of(j * TK, TK), TK)


def _attn_scratch(heads, score_buffers):
    return [pltpu.VMEM((heads, TK, TQ), F32)] * score_buffers + [pltpu.VMEM((heads, HEAD_DIM, TQ), F32)]


def _fox_kernel(qt_ref, k_ref, vt_ref, o_ref, sa_ref, sb_ref, acc_ref, *, seq):
    assert TQ == 2 * TK
    r = lax.broadcasted_iota(jnp.int32, (TK, TQ), 0)
    c = lax.broadcasted_iota(jnp.int32, (TK, TQ), 1)
    causal = r <= c
    causal_late = (lax.broadcasted_iota(jnp.int32, (TK, TK), 0)
                   <= lax.broadcasted_iota(jnp.int32, (TK, TK), 1))
    late = slice(TK, TQ)

    for qi in range(seq // TQ):
        qt = [qt_ref[hh, qi] for hh in range(FOX_HEADS_PER_STEP)]

        def produce(j, buf, cols=slice(None)):
            for hh in range(FOX_HEADS_PER_STEP):
                buf[hh, :, cols] = _dot(k_ref[_chunk_rows(j), hh * LANES:(hh + 1) * LANES], qt[hh][:, cols])

        def consume(j, buf, state, mask, cols=slice(None)):
            out = []
            for hh in range(FOX_HEADS_PER_STEP):
                vt = vt_ref[hh // 2, j]
                m_all, l_all = state[2 * hh:2 * hh + 2]
                m_prev, l_prev = m_all[:, cols], l_all[:, cols]
                s = buf[hh, :, cols]
                if mask is not None:
                    s = jnp.where(mask, s, NEG)
                m_new = jnp.maximum(m_prev, jnp.max(s, axis=0, keepdims=True))
                alpha = jnp.exp2(m_prev - m_new)
                p = jnp.exp2(s - m_new)
                l_new = alpha * l_prev + jnp.sum(p, axis=0, keepdims=True)
                pv = _dot(vt[(hh % 2) * HEAD_DIM:(hh % 2 + 1) * HEAD_DIM, :], p.astype(BF16))
                acc_ref[hh, :, cols] = alpha * acc_ref[hh, :, cols] + pv
                if cols != slice(None):
                    m_new = jnp.concatenate([m_all[:, :cols.start], m_new], axis=1)
                    l_new = jnp.concatenate([l_all[:, :cols.start], l_new], axis=1)
                out += [m_new, l_new]
            return tuple(out)

        n_full = qi * (TQ // TK)
        acc_ref[...] = jnp.zeros_like(acc_ref)
        init = (jnp.full((1, TQ), NEG, F32), jnp.zeros((1, TQ), F32))
        st = init * FOX_HEADS_PER_STEP
        produce(0, sa_ref)
        if n_full > 0:
            def body(i, st):
                j = 2 * i
                produce(j + 1, sb_ref)
                st = consume(j, sa_ref, st, None)
                produce(j + 2, sa_ref)
                return consume(j + 1, sb_ref, st, None)
            st = lax.fori_loop(0, n_full // 2, body, st)
        produce(n_full + 1, sb_ref, late)
        st = consume(n_full, sa_ref, st, causal)
        st = consume(n_full + 1, sb_ref, st, causal_late, late)
        for pp in range(FOX_HEADS_PER_STEP // 2):
            h0, h1 = 2 * pp, 2 * pp + 1
            out_t = jnp.concatenate([acc_ref[h0] * (1.0 / st[2 * h0 + 1]), acc_ref[h1] * (1.0 / st[2 * h1 + 1])],
                                    axis=0)
            o_ref[qi * TQ:(qi + 1) * TQ, pp * PAIR:(pp + 1) * PAIR] = jnp.transpose(out_t).astype(BF16)


def _fox_attention(fqt, fk, fvt, batch, seq):
    n = fk.shape[0]
    hps = FOX_HEADS_PER_STEP
    blk = lambda b, g: (b, g, 0, 0, 0)
    return pl.pallas_call(
        functools.partial(_fox_kernel, seq=seq),
        grid=(batch, N_HEADS_FOX // hps),
        in_specs=[pl.BlockSpec((None, hps, seq // TQ, LANES, TQ), blk),
                  pl.BlockSpec((seq, hps * LANES), lambda b, g: (b, g)),
                  pl.BlockSpec((None, hps // 2, seq // TK, LANES, TK), blk)],
        out_specs=pl.BlockSpec((seq, hps * HEAD_DIM), lambda b, g: (b, g)),
        out_shape=jax.ShapeDtypeStruct((n, W_FOX), BF16),
        scratch_shapes=_attn_scratch(heads=hps, score_buffers=2),
        compiler_params=_params(("arbitrary", "arbitrary")),
        name="fox_attn",
    )(fqt, fk, fvt)


def _sb_kernel(qt_ref, k_ref, vt_ref, o_ref, z0_ref, z1_ref, z2_ref, s0_ref, s1_ref, acc_ref, *, seq):
    assert TQ == 2 * TK
    lo = _lane_lo()
    r = lax.broadcasted_iota(jnp.int32, (TK, TQ), 0)
    c = lax.broadcasted_iota(jnp.int32, (TK, TQ), 1)
    strict = r < c
    rk = lax.broadcasted_iota(jnp.int32, (TK, TK), 0)
    ck = lax.broadcasted_iota(jnp.int32, (TK, TK), 1)
    strict_late = rk < ck
    late = slice(TK, TQ)
    tri = jnp.where(rk <= ck, 1.0, 0.0).astype(BF16)
    zero = jnp.zeros((), BF16)

    zbufs, sbufs = (z0_ref, z1_ref, z2_ref), (s0_ref, s1_ref)

    for qi in range(seq // TQ):
        qt = qt_ref[qi]
        n_full = qi * (TQ // TK)
        chunks = ([(n_full + 1, strict_late, late), (n_full, strict, slice(None))]
                  + [(j, None, slice(None)) for j in reversed(range(n_full))])

        def produce(chunk, zb):
            j, _, cols = chunk
            kc = k_ref[j * TK:(j + 1) * TK, :]
            zb[0, :, cols] = _dot(jnp.where(lo, kc, zero), qt[:, cols])
            zb[1, :, cols] = _dot(jnp.where(lo, zero, kc), qt[:, cols])

        def sum_stage(chunk, zb, sb):
            _, mask, cols = chunk
            for hh in range(2):
                z = zb[hh, :, cols].astype(BF16)
                sp = jnp.maximum(z, 0) + jnp.log(1 + jnp.exp(-jnp.abs(z)))
                if mask is not None:
                    sp = jnp.where(mask, sp, jnp.zeros((), BF16))
                sb[hh, :, cols] = _dot(tri, sp)

        def weight_stage(chunk, zb, sb, runs):
            j, mask, cols = chunk
            vt = vt_ref[j]
            out = []
            for hh in range(2):
                suffix = sb[hh, :, cols]
                a = jnp.exp(zb[hh, :, cols] - suffix)
                if mask is not None:
                    a = jnp.where(mask, a, 0.0)
                pv = _dot(vt[hh * HEAD_DIM:(hh + 1) * HEAD_DIM, :], a.astype(BF16))
                acc_ref[hh, :, cols] += jnp.exp(-runs[hh][:, cols]) * pv
                run = runs[hh][:, cols] + suffix[0:1, :]
                if cols != slice(None):
                    run = jnp.concatenate([runs[hh][:, :cols.start], run], axis=1)
                out.append(run)
            return tuple(out)

        acc_ref[...] = jnp.zeros_like(acc_ref)
        runs = (jnp.zeros((1, TQ), F32),) * 2
        n = len(chunks)
        produce(chunks[0], zbufs[0])
        produce(chunks[1], zbufs[1])
        sum_stage(chunks[0], zbufs[0], sbufs[0])
        for i in range(n):
            if i + 2 < n:
                produce(chunks[i + 2], zbufs[(i + 2) % 3])
            if i + 1 < n:
                sum_stage(chunks[i + 1], zbufs[(i + 1) % 3], sbufs[(i + 1) % 2])
            runs = weight_stage(chunks[i], zbufs[i % 3], sbufs[i % 2], runs)
        out_t = jnp.concatenate([acc_ref[0], acc_ref[1]], axis=0)
        o_ref[qi * TQ:(qi + 1) * TQ, :] = jnp.transpose(out_t).astype(BF16)


def _sb_attention(sqt, sbk, svt, batch, seq):
    n = sbk.shape[0]
    npair = W_SB // PAIR
    tspec = lambda w: pl.BlockSpec((None, None, seq // w, LANES, w), lambda b, p: (b, p, 0, 0, 0))
    return pl.pallas_call(
        functools.partial(_sb_kernel, seq=seq),
        grid=(batch, npair),
        in_specs=[tspec(TQ), pl.BlockSpec((seq, PAIR), lambda b, p: (b, p)), tspec(TK)],
        out_specs=pl.BlockSpec((seq, PAIR), lambda b, p: (b, p)),
        out_shape=jax.ShapeDtypeStruct((n, W_SB), BF16),
        scratch_shapes=_attn_scratch(heads=2, score_buffers=5),
        compiler_params=_params(("arbitrary", "arbitrary")),
        name="sb_attn",
    )(sqt, sbk, svt)


def _dil_kernel(q_ref, k_ref, v_ref, o_ref, lse_ref, sa_ref, sb_ref, *, n, dil):
    lo = _lane_lo()
    keeps = (lo, jnp.logical_not(lo))
    r = lax.broadcasted_iota(jnp.int32, (DQ, 2 * DQ), 0)
    c = lax.broadcasted_iota(jnp.int32, (DQ, 2 * DQ), 1)
    band = (c >= r) & (c <= r + DQ)
    causal = (lax.broadcasted_iota(jnp.int32, (DQ, DQ), 1)
              <= lax.broadcasted_iota(jnp.int32, (DQ, DQ), 0))
    zero = jnp.zeros((), BF16)
    npair = W_DIL // PAIR

    def key_rows(blk):
        return slice(0, DQ) if blk == 0 else slice((blk - 1) * DQ, (blk + 1) * DQ)

    def produce(unit, buf):
        stream, blk = unit
        rows, krows = slice(blk * DQ, (blk + 1) * DQ), key_rows(blk)
        for pp in range(npair):
            cols = slice(stream * W_DIL + pp * PAIR, stream * W_DIL + (pp + 1) * PAIR)
            q, k = q_ref[rows, cols], k_ref[krows, cols]
            for hh, keep in enumerate(keeps):
                buf[2 * pp + hh, :, 0:krows.stop - krows.start] = _dot_nt(jnp.where(keep, q, zero), k)

    def consume(unit, buf):
        stream, blk = unit
        rows, krows = slice(blk * DQ, (blk + 1) * DQ), key_rows(blk)
        nk = krows.stop - krows.start
        mask = causal if blk == 0 else band
        for pp in range(npair):
            cols = slice(stream * W_DIL + pp * PAIR, stream * W_DIL + (pp + 1) * PAIR)
            v = v_ref[krows, cols]
            outs, lses = [], []
            for hh, keep in enumerate(keeps):
                s = jnp.where(mask, buf[2 * pp + hh, :, 0:nk], NEG)
                m = jnp.max(s, axis=-1, keepdims=True)
                p = jnp.exp(s - m)
                den = jnp.sum(p, axis=-1, keepdims=True)
                o = _dot(p.astype(BF16), jnp.where(keep, v, zero))
                outs.append(o * (1.0 / den))
                lses.append(m + jnp.log(den))
            o_ref[rows, cols] = (outs[0] + outs[1]).astype(BF16)
            lse_ref[rows, cols] = jnp.where(lo, lses[0], lses[1])

    units = [(stream, blk) for stream in range(dil) for blk in range(n // DQ)]
    bufs = (sa_ref, sb_ref)
    produce(units[0], bufs[0])
    for u, unit in enumerate(units):
        if u + 1 < len(units):
            produce(units[u + 1], bufs[(u + 1) % 2])
        consume(unit, bufs[u % 2])


def _dil_attention(dq, dk, dv, group, batch, seq):
    window, dil = DIL_PATTERNS[group]
    assert window // dil == DQ
    n = seq // dil
    spec = pl.BlockSpec((None, n, dil * W_DIL), lambda b: (b, 0, 0))
    return pl.pallas_call(
        functools.partial(_dil_kernel, n=n, dil=dil),
        grid=(batch,),
        in_specs=[spec, spec, spec],
        out_specs=[spec, spec],
        out_shape=[jax.ShapeDtypeStruct((batch, n, dil * W_DIL), BF16),
                   jax.ShapeDtypeStruct((batch, n, dil * W_DIL), F32)],
        scratch_shapes=[pltpu.VMEM((N_HEADS_DIL, DQ, 2 * DQ), F32)] * 2,
        compiler_params=_params(("arbitrary",)),
        name=f"dil_attn_g{group}",
    )(dq, dk, dv)


def _merge_kernel(x_ref, oa_ref, ob_ref, o0_ref, o1_ref, o2_ref, l0_ref, l1_ref, l2_ref, gate_ref,
                  wa_ref, wb_ref, wc_ref, wo_ref, out_ref, stage_ref):
    def tokens(ref, g):
        dil = DIL_PATTERNS[g][1]
        if dil == 1:
            return ref[...].astype(F32)
        npair = W_DIL // PAIR
        for s in range(dil):
            for pp in range(npair):
                col = s * W_DIL + pp * PAIR
                stage_ref[g * npair + pp, pl.ds(s, TM // dil, stride=dil), :] = ref[:, col:col + PAIR].astype(F32)
        return jnp.concatenate([stage_ref[g * npair + pp] for pp in range(npair)], axis=1)

    lses = [tokens(ref, g) for g, ref in enumerate((l0_ref, l1_ref, l2_ref))]
    mx = jnp.maximum(jnp.maximum(lses[0], lses[1]), lses[2])
    es = [jnp.exp(l - mx) for l in lses]
    inv = 1.0 / (es[0] + es[1] + es[2])
    out_c = None
    for g, ref in enumerate((o0_ref, o1_ref, o2_ref)):
        term = (es[g] * inv) * tokens(ref, g)
        out_c = term if out_c is None else out_c + term
    ys = (_dot(oa_ref[...], wa_ref[...]), _dot(ob_ref[...], wb_ref[...]), _dot(out_c.astype(BF16), wc_ref[...]))
    merged = None
    for b in range(N_BRANCHES):
        gate = 1.0 / (1.0 + jnp.exp(-gate_ref[:, b * D_MODEL:(b + 1) * D_MODEL].astype(F32)))
        merged = gate * ys[b] if merged is None else merged + gate * ys[b]
    out_ref[...] = x_ref[...] + _dot(merged.astype(BF16), wo_ref[...])


def _merge(x2, out_a, out_b, os_, lses, gates, wa, wb, wc, wo, seq):
    n = x2.shape[0]
    nt = seq // TM
    row = lambda i: (i, 0)
    tile = lambda w: pl.BlockSpec((TM, w), row)
    streams = [pl.BlockSpec((None, TM // d, d * W_DIL), lambda i: (i // nt, i % nt, 0)) for _, d in DIL_PATTERNS]
    return pl.pallas_call(
        _merge_kernel,
        grid=(n // TM,),
        in_specs=[tile(D_MODEL), tile(W_FOX), tile(W_SB)] + streams * 2 + [tile(SZ_GATES)]
                 + [_resident(w.shape) for w in (wa, wb, wc, wo)],
        out_specs=tile(D_MODEL),
        out_shape=jax.ShapeDtypeStruct((n, D_MODEL), F32),
        scratch_shapes=[pltpu.VMEM((W_DIL_ALL // PAIR, TM, LANES), F32)],
        compiler_params=_params(("arbitrary",)),
        name="merge",
    )(x2, out_a, out_b, *os_, *lses, gates, wa, wb, wc, wo)


def _mlp_kernel(x_ref, g_ref, w1_ref, w2_ref, out_ref):
    x = x_ref[...]
    ms = jnp.mean(x * x, axis=-1, keepdims=True)
    h = (x * lax.rsqrt(ms + EPS) * g_ref[...]).astype(BF16)
    acc = x
    for c in range(0, D_FF, FF_CHUNK):
        u = jnp.maximum(_dot(h, w1_ref[:, c:c + FF_CHUNK]), 0.0)
        acc = acc + _dot((u * u).astype(BF16), w2_ref[c:c + FF_CHUNK, :])
    out_ref[...] = acc


def _mlp(x2, g, w1, w2):
    n = x2.shape[0]
    row = lambda i: (i, 0)
    return pl.pallas_call(
        _mlp_kernel,
        grid=(n // TM,),
        in_specs=[pl.BlockSpec((TM, D_MODEL), row), _resident((1, D_MODEL)),
                  _resident(w1.shape), _resident(w2.shape)],
        out_specs=pl.BlockSpec((TM, D_MODEL), row),
        out_shape=jax.ShapeDtypeStruct((n, D_MODEL), F32),
        compiler_params=_params(("arbitrary",)),
        name="mlp",
    )(x2, g, w1, w2)


def _prep_tables():
    half = HEAD_DIM // 2
    lane = np.arange(LANES)
    inv = (1.0 / (ROPE_THETA ** (np.arange(half, dtype=np.float32) / half))).astype(np.float32)
    first = (lane % HEAD_DIM) < half
    rope_tab = jnp.asarray(np.stack([np.where(first, -1.0, 1.0), first.astype(np.float32)]), F32)
    invf = jnp.asarray(np.repeat(inv[:, None], TM, axis=1), F32)
    tri = jnp.asarray(np.tril(np.ones((TM, TM), np.float32)), BF16)
    return (rope_tab, invf, tri) + _aug_tables()


def kernel(x, positions, attn_norm, w_in, b_forget, q_norm_fox, k_norm_fox, q_norm_dil, k_norm_dil,
           w_up_fox, w_up_sb, w_up_dil, w_out, mlp_norm, w_mlp_in, w_mlp_out):
    batch, seq, _ = x.shape
    depth = w_in.shape[0]
    n = batch * seq
    o1 = SZ_FOX_QKV
    o2 = o1 + SZ_FORGET
    w_forget = jnp.pad(w_in[:, :, o1:o2], ((0, 0), (0, 0), (0, LANES - SZ_FORGET)))
    w_all = jnp.concatenate([w_in[:, :, :o1], w_in[:, :, o2:], w_forget], axis=2).astype(BF16)
    wa, wb, wc, wo = (w.astype(BF16) for w in (w_up_fox, w_up_sb, w_up_dil, w_out))
    w1, w2 = w_mlp_in.astype(BF16), w_mlp_out.astype(BF16)
    pair = lambda g: jnp.tile(g.astype(F32), (1, LANES // HEAD_DIM))
    gains = jnp.stack([pair(q_norm_fox) * (SCALE * LOG2E), pair(k_norm_fox),
                       pair(q_norm_dil) * SCALE, pair(k_norm_dil)], axis=1)
    bf = jnp.pad(b_forget.astype(F32), ((0, 0), (0, LANES - SZ_FORGET)))[:, None, :]
    tables = _prep_tables()
    pos = positions.reshape(n // TM, 1, TM)

    x2 = x.reshape(n, D_MODEL)
    for l in range(depth):
        front = _front(x2, attn_norm[l][None, :], w_all[l], pos, gains[l], bf[l], tables, batch, seq)
        fqt, fk, fvt, sqt, sbk, svt = front[:6]
        dqs, dks, dvs, gates = front[6:9], front[9:12], front[12:15], front[15]
        out_a = _fox_attention(fqt, fk, fvt, batch, seq)
        out_b = _sb_attention(sqt, sbk, svt, batch, seq)
        dil = [_dil_attention(dqs[g], dks[g], dvs[g], g, batch, seq) for g in range(N_DIL_GROUPS)]
        x2 = _merge(x2, out_a, out_b, [d[0] for d in dil], [d[1] for d in dil], gates,
                    wa[l], wb[l], wc[l], wo[l], seq)
        x2 = _mlp(x2, mlp_norm[l][None, :], w1[l], w2[l])
    return x2.reshape(batch, seq, D_MODEL)
```

```python
import functools
import math

import numpy as np
import jax
import jax.numpy as jnp
from jax import lax
from jax.experimental import pallas as pl
from jax.experimental.pallas import tpu as pltpu

D_MODEL = 1024
HEAD_DIM = 64
N_HEADS_FOX = 8
N_HEADS_SB = 8
DIL_PATTERNS = ((128, 1), (512, 4), (2048, 16))
N_DIL_GROUPS = len(DIL_PATTERNS)
N_HEADS_DIL = 4
D_FF = 4 * D_MODEL
ROPE_THETA = 10000.0
EPS = 1e-6
N_BRANCHES = 3

W_FOX = N_HEADS_FOX * HEAD_DIM
W_SB = N_HEADS_SB * HEAD_DIM
W_DIL = N_HEADS_DIL * HEAD_DIM
W_DIL_ALL = N_DIL_GROUPS * W_DIL
SZ_FOX_QKV = 3 * W_FOX
SZ_FORGET = N_HEADS_FOX
SZ_SB_QKV = 3 * W_SB
SZ_DIL_QKV = 3 * W_DIL_ALL
SZ_GATES = N_BRANCHES * D_MODEL

LANES = 128
PAIR = LANES
SCALE = 1.0 / math.sqrt(HEAD_DIM)
LOG2E = math.log2(math.e)
NEG = -1e30

TM = 512
TQ = 512
TK = 256
FOX_HEADS_PER_STEP = 4
DQ = 128
PROJ_CHUNK = 768
FF_CHUNK = 1024
VMEM_LIMIT = 56 * 1024 * 1024

AUG_F = 0
AUG_ONE = 3

F32 = jnp.float32
BF16 = jnp.bfloat16


def _params(sem):
    return pltpu.CompilerParams(dimension_semantics=sem, vmem_limit_bytes=VMEM_LIMIT)


def _resident(shape):
    nd = len(shape)
    return pl.BlockSpec(shape, lambda *_: (0,) * nd, pipeline_mode=pl.Buffered(1))


def _dot(a, b):
    return jnp.dot(a, b, preferred_element_type=F32)


def _dot_nt(a, b):
    return lax.dot_general(a, b, (((1,), (1,)), ((), ())), preferred_element_type=F32)


def _split3(x):
    hi = x.astype(BF16)
    r1 = x - hi.astype(F32)
    mid = r1.astype(BF16)
    lo = (r1 - mid.astype(F32)).astype(BF16)
    return hi, mid, lo


def _lane_lo():
    return lax.broadcasted_iota(jnp.int32, (1, LANES), 1) < HEAD_DIM


OFF_FOX = 0
OFF_SB = OFF_FOX + SZ_FOX_QKV
OFF_DIL = OFF_SB + SZ_SB_QKV
OFF_GATES = OFF_DIL + SZ_DIL_QKV
OFF_FORGET = OFF_GATES + SZ_GATES
MXU_COLS = 256


def _head_rms(y, head_sum):
    sq = (y * y).astype(BF16)
    ss = jnp.concatenate([_dot(sq[:, c:c + MXU_COLS], head_sum) for c in range(0, y.shape[1], MXU_COLS)], axis=1)
    return y * lax.rsqrt(ss * (1.0 / HEAD_DIM) + EPS)


def _put_transposed(out_ref, idx, tile, width):
    tt = jnp.transpose(tile).astype(BF16)
    for i in range(TM // width):
        out_ref[idx, i] = tt[:, i * width:(i + 1) * width]


def _front_kernel(x_ref, g_ref, w_fox_ref, w_rest_ref, w_forget_ref, pos_ref, gains_ref, bf_ref,
                  rope_ref, invf_ref, tri_ref, eaug_ref, ones_ref,
                  fqt_out, fk_out, fvt_out, sqt_out, sbk_out, svt_out,
                  dq0, dq1, dq2, dk0, dk1, dk2, dv0, dv1, dv2, gate_out, carry_ref, stage_ref):
    t = pl.program_id(1)
    lo = _lane_lo()
    gq_fox, gk_fox, gq_dil, gk_dil = (gains_ref[i:i + 1, :] for i in range(4))

    x = x_ref[...]
    ms = jnp.mean(x * x, axis=-1, keepdims=True)
    h = (x * lax.rsqrt(ms + EPS) * g_ref[...]).astype(BF16)

    def proj(col, width):
        if col >= OFF_FORGET:
            ref, col = w_forget_ref, col - OFF_FORGET
        elif col >= OFF_SB:
            ref, col = w_rest_ref, col - OFF_SB
        else:
            ref = w_fox_ref
        return _dot(h, ref[:, col:col + width])

    gate_cols = iter(range(0, SZ_GATES, MXU_COLS))

    def gate_step():
        c = next(gate_cols, None)
        if c is not None:
            gate_out[:, c:c + MXU_COLS] = proj(OFF_GATES + c, MXU_COLS).astype(BF16)
        return c is not None

    @pl.when(t == 0)
    def _():
        carry_ref[...] = jnp.zeros_like(carry_ref)

    z = proj(OFF_FORGET, LANES) + bf_ref[...]
    log_f = jnp.minimum(z, 0.0) - jnp.log(1.0 + jnp.exp(-jnp.abs(z)))
    tri = tri_ref[...]
    hi, mid, lw = _split3(log_f)
    cum3 = _dot(tri, jnp.concatenate([hi, mid, lw], axis=1))
    f_cum = (cum3[:, :LANES] + cum3[:, LANES:2 * LANES] + cum3[:, 2 * LANES:]) + carry_ref[0:1, :]
    carry_ref[...] = jnp.broadcast_to(f_cum[TM - 1:TM, :], carry_ref.shape)
    hi, mid, lw = _split3(f_cum * LOG2E)
    is_head = lax.broadcasted_iota(jnp.int32, (1, LANES), 1) < N_HEADS_FOX
    mid_lo = (jnp.where(is_head, mid.astype(F32), 0.0)
              + pltpu.roll(jnp.where(is_head, lw.astype(F32), 0.0), N_HEADS_FOX, 1)).astype(BF16)
    aug = _dot(jnp.concatenate([hi, mid_lo], axis=1), eaug_ref[...]) + ones_ref[...]

    hr = lax.broadcasted_iota(jnp.int32, (MXU_COLS, MXU_COLS), 0) // HEAD_DIM
    hc = lax.broadcasted_iota(jnp.int32, (MXU_COLS, MXU_COLS), 1) // HEAD_DIM
    head_sum = jnp.where(hr == hc, 1.0, 0.0).astype(BF16)

    fox_q = _head_rms(proj(OFF_FOX, W_FOX), head_sum)
    fox_k = _head_rms(proj(OFF_FOX + W_FOX, W_FOX), head_sum)
    for p in range(W_FOX // PAIR):
        sl = slice(p * PAIR, (p + 1) * PAIR)
        qn = fox_q[:, sl] * gq_fox
        kn = fox_k[:, sl] * gk_fox
        for hh in range(2):
            hd = 2 * p + hh
            own = lo if hh == 0 else jnp.logical_not(lo)
            _put_transposed(fqt_out, hd, jnp.where(own, qn, aug[:, hd * LANES:(hd + 1) * LANES]), TQ)
            ka = aug[:, (N_HEADS_FOX + hd) * LANES:(N_HEADS_FOX + hd + 1) * LANES]
            fk_out[:, hd * LANES:(hd + 1) * LANES] = jnp.where(own, kn, ka).astype(BF16)
        gate_step()
    for out, col, width, mult in ((fvt_out, OFF_FOX + 2 * W_FOX, TK, None), (sqt_out, OFF_SB, TQ, SCALE),
                                  (svt_out, OFF_SB + 2 * W_SB, TK, None)):
        y = proj(col, W_SB)
        for p in range(W_SB // PAIR):
            tile = y[:, p * PAIR:(p + 1) * PAIR]
            _put_transposed(out, p, tile if mult is None else tile * mult, width)
        gate_step()
    sbk_out[...] = proj(OFF_SB + W_SB, W_SB).astype(BF16)

    ang_t = pos_ref[...].astype(F32) * invf_ref[...]
    reps = LANES // (HEAD_DIM // 2)
    cos = jnp.transpose(jnp.concatenate([jnp.cos(ang_t)] * reps, axis=0))
    sin_signed = jnp.transpose(jnp.concatenate([jnp.sin(ang_t)] * reps, axis=0)) * rope_ref[0:1, :]
    first_half = rope_ref[1:2, :] > 0.5

    def rope(x):
        partner = jnp.where(first_half, pltpu.roll(x, LANES - HEAD_DIM // 2, 1), pltpu.roll(x, HEAD_DIM // 2, 1))
        return x * cos + partner * sin_signed

    npair_dil = W_DIL // PAIR

    def put_streams(outs):
        for g, (_, dil) in enumerate(DIL_PATTERNS):
            for s in range(dil):
                rows = pl.ds(s, TM // dil, stride=dil) if dil > 1 else slice(None)
                for pp in range(npair_dil):
                    col = s * W_DIL + pp * PAIR
                    outs[g][:, col:col + PAIR] = stage_ref[g * npair_dil + pp, rows, :].astype(BF16)

    for i, (gain, outs) in enumerate(((gq_dil, (dq0, dq1, dq2)), (gk_dil, (dk0, dk1, dk2)), (None, (dv0, dv1, dv2)))):
        y = proj(OFF_DIL + i * W_DIL_ALL, W_DIL_ALL)
        if gain is not None:
            y = _head_rms(y, head_sum)
        for p in range(W_DIL_ALL // PAIR):
            tile = y[:, p * PAIR:(p + 1) * PAIR]
            stage_ref[p] = tile if gain is None else rope(tile * gain)
            if p % 3 == 2:
                gate_step()
        put_streams(outs)
    while gate_step():
        pass


def _aug_tables():
    e = np.zeros((2 * LANES, 2 * N_HEADS_FOX * LANES), np.float32)
    ones = np.zeros((1, 2 * N_HEADS_FOX * LANES), np.float32)
    for h in range(N_HEADS_FOX):
        other = HEAD_DIM if h % 2 == 0 else 0
        qbase, kbase = h * LANES + other, (N_HEADS_FOX + h) * LANES + other
        for part in range(3):
            row = h if part == 0 else LANES + (part - 1) * N_HEADS_FOX + h
            e[row, qbase + AUG_F + part] = 1.0
            ones[0, kbase + AUG_F + part] = 1.0
            e[row, kbase + AUG_ONE + part] = -1.0
            ones[0, qbase + AUG_ONE + part] = 1.0
    return jnp.asarray(e, BF16), jnp.asarray(ones, F32)


def _front(x2, g, weights, pos, gains, bf, tables, batch, seq):
    n = x2.shape[0]
    nt = seq // TM
    row = lambda b, t: (b * nt + t, 0)
    tchunk = lambda heads, w: pl.BlockSpec((None, heads, TM // w, LANES, w), lambda b, t: (b, 0, t, 0, 0))
    tshape = lambda heads, w: jax.ShapeDtypeStruct((batch, heads, seq // w, LANES, w), BF16)
    stream_specs = [pl.BlockSpec((None, TM // d, d * W_DIL), lambda b, t: (b, t, 0)) for _, d in DIL_PATTERNS] * 3
    stream_shapes = [jax.ShapeDtypeStruct((batch, seq // d, d * W_DIL), BF16) for _, d in DIL_PATTERNS] * 3
    npair = W_FOX // PAIR
    return pl.pallas_call(
        _front_kernel,
        grid=(batch, nt),
        in_specs=[pl.BlockSpec((TM, D_MODEL), row), _resident((1, D_MODEL))]
                 + [_resident(w.shape) for w in weights]
                 + [pl.BlockSpec((None, 1, TM), lambda b, t: (b * nt + t, 0, 0)),
                  _resident((4, LANES)), _resident((1, LANES))]
                 + [_resident(a.shape) for a in tables],
        out_specs=[tchunk(N_HEADS_FOX, TQ), pl.BlockSpec((TM, N_HEADS_FOX * LANES), row), tchunk(npair, TK),
                   tchunk(npair, TQ), pl.BlockSpec((TM, W_SB), row), tchunk(npair, TK)] + stream_specs
                  + [pl.BlockSpec((TM, SZ_GATES), row)],
        out_shape=[tshape(N_HEADS_FOX, TQ), jax.ShapeDtypeStruct((n, N_HEADS_FOX * LANES), BF16), tshape(npair, TK),
                   tshape(npair, TQ), jax.ShapeDtypeStruct((n, W_SB), BF16), tshape(npair, TK)] + stream_shapes
                  + [jax.ShapeDtypeStruct((n, SZ_GATES), BF16)],
        scratch_shapes=[pltpu.VMEM((8, LANES), F32), pltpu.VMEM((W_DIL_ALL // PAIR, TM, LANES), F32)],
        compiler_params=_params(("arbitrary", "arbitrary")),
        name="front",
    )(x2, g, *weights, pos, gains, bf, *tables)


def _chunk_rows(j):
    return pl.ds(j * TK if isinstance(j, int) else pl.multiple_of(j * TK, TK), TK)


def _attn_scratch(heads, score_buffers):
    return [pltpu.VMEM((heads, TK, TQ), F32)] * score_buffers + [pltpu.VMEM((heads, HEAD_DIM, TQ), F32)]


def _fox_kernel(qt_ref, k_ref, vt_ref, o_ref, sa_ref, sb_ref, acc_ref, *, seq):
    assert TQ == 2 * TK
    r = lax.broadcasted_iota(jnp.int32, (TK, TQ), 0)
    c = lax.broadcasted_iota(jnp.int32, (TK, TQ), 1)
    causal = r <= c
    causal_late = (lax.broadcasted_iota(jnp.int32, (TK, TK), 0)
                   <= lax.broadcasted_iota(jnp.int32, (TK, TK), 1))
    late = slice(TK, TQ)

    for qi in range(seq // TQ):
        qt = [qt_ref[hh, qi] for hh in range(FOX_HEADS_PER_STEP)]

        def produce(j, buf, cols=slice(None)):
            for hh in range(FOX_HEADS_PER_STEP):
                buf[hh, :, cols] = _dot(k_ref[_chunk_rows(j), hh * LANES:(hh + 1) * LANES], qt[hh][:, cols])

        def consume(j, buf, state, mask, cols=slice(None)):
            out = []
            for hh in range(FOX_HEADS_PER_STEP):
                vt = vt_ref[hh // 2, j]
                m_all, l_all = state[2 * hh:2 * hh + 2]
                m_prev, l_prev = m_all[:, cols], l_all[:, cols]
                s = buf[hh, :, cols]
                if mask is not None:
                    s = jnp.where(mask, s, NEG)
                m_new = jnp.maximum(m_prev, jnp.max(s, axis=0, keepdims=True))
                alpha = jnp.exp2(m_prev - m_new)
                p = jnp.exp2(s - m_new)
                l_new = alpha * l_prev + jnp.sum(p, axis=0, keepdims=True)
                pv = _dot(vt[(hh % 2) * HEAD_DIM:(hh % 2 + 1) * HEAD_DIM, :], p.astype(BF16))
                acc_ref[hh, :, cols] = alpha * acc_ref[hh, :, cols] + pv
                if cols != slice(None):
                    m_new = jnp.concatenate([m_all[:, :cols.start], m_new], axis=1)
                    l_new = jnp.concatenate([l_all[:, :cols.start], l_new], axis=1)
                out += [m_new, l_new]
            return tuple(out)

        n_full = qi * (TQ // TK)
        acc_ref[...] = jnp.zeros_like(acc_ref)
        init = (jnp.full((1, TQ), NEG, F32), jnp.zeros((1, TQ), F32))
        st = init * FOX_HEADS_PER_STEP
        produce(0, sa_ref)
        if n_full > 0:
            def body(i, st):
                j = 2 * i
                produce(j + 1, sb_ref)
                st = consume(j, sa_ref, st, None)
                produce(j + 2, sa_ref)
                return consume(j + 1, sb_ref, st, None)
            st = lax.fori_loop(0, n_full // 2, body, st)
        produce(n_full + 1, sb_ref, late)
        st = consume(n_full, sa_ref, st, causal)
        st = consume(n_full + 1, sb_ref, st, causal_late, late)
        for pp in range(FOX_HEADS_PER_STEP // 2):
            h0, h1 = 2 * pp, 2 * pp + 1
            out_t = jnp.concatenate([acc_ref[h0] * (1.0 / st[2 * h0 + 1]), acc_ref[h1] * (1.0 / st[2 * h1 + 1])],
                                    axis=0)
            o_ref[qi * TQ:(qi + 1) * TQ, pp * PAIR:(pp + 1) * PAIR] = jnp.transpose(out_t).astype(BF16)


def _fox_attention(fqt, fk, fvt, batch, seq):
    n = fk.shape[0]
    hps = FOX_HEADS_PER_STEP
    blk = lambda b, g: (b, g, 0, 0, 0)
    return pl.pallas_call(
        functools.partial(_fox_kernel, seq=seq),
        grid=(batch, N_HEADS_FOX // hps),
        in_specs=[pl.BlockSpec((None, hps, seq // TQ, LANES, TQ), blk),
                  pl.BlockSpec((seq, hps * LANES), lambda b, g: (b, g)),
                  pl.BlockSpec((None, hps // 2, seq // TK, LANES, TK), blk)],
        out_specs=pl.BlockSpec((seq, hps * HEAD_DIM), lambda b, g: (b, g)),
        out_shape=jax.ShapeDtypeStruct((n, W_FOX), BF16),
        scratch_shapes=_attn_scratch(heads=hps, score_buffers=2),
        compiler_params=_params(("arbitrary", "arbitrary")),
        name="fox_attn",
    )(fqt, fk, fvt)


def _sb_kernel(qt_ref, k_ref, vt_ref, o_ref, z0_ref, z1_ref, z2_ref, s0_ref, s1_ref, acc_ref, *, seq):
    assert TQ == 2 * TK
    lo = _lane_lo()
    r = lax.broadcasted_iota(jnp.int32, (TK, TQ), 0)
    c = lax.broadcasted_iota(jnp.int32, (TK, TQ), 1)
    strict = r < c
    rk = lax.broadcasted_iota(jnp.int32, (TK, TK), 0)
    ck = lax.broadcasted_iota(jnp.int32, (TK, TK), 1)
    strict_late = rk < ck
    late = slice(TK, TQ)
    tri = jnp.where(rk <= ck, 1.0, 0.0).astype(BF16)
    zero = jnp.zeros((), BF16)

    zbufs, sbufs = (z0_ref, z1_ref, z2_ref), (s0_ref, s1_ref)

    for qi in range(seq // TQ):
        qt = qt_ref[qi]
        n_full = qi * (TQ // TK)
        chunks = ([(n_full + 1, strict_late, late), (n_full, strict, slice(None))]
                  + [(j, None, slice(None)) for j in reversed(range(n_full))])

        def produce(chunk, zb):
            j, _, cols = chunk
            kc = k_ref[j * TK:(j + 1) * TK, :]
            zb[0, :, cols] = _dot(jnp.where(lo, kc, zero), qt[:, cols])
            zb[1, :, cols] = _dot(jnp.where(lo, zero, kc), qt[:, cols])

        def sum_stage(chunk, zb, sb):
            _, mask, cols = chunk
            for hh in range(2):
                z = zb[hh, :, cols].astype(BF16)
                sp = jnp.maximum(z, 0) + jnp.log(1 + jnp.exp(-jnp.abs(z)))
                if mask is not None:
                    sp = jnp.where(mask, sp, jnp.zeros((), BF16))
                sb[hh, :, cols] = _dot(tri, sp)

        def weight_stage(chunk, zb, sb, runs):
            j, mask, cols = chunk
            vt = vt_ref[j]
            out = []
            for hh in range(2):
                suffix = sb[hh, :, cols]
                a = jnp.exp(zb[hh, :, cols] - suffix)
                if mask is not None:
                    a = jnp.where(mask, a, 0.0)
                pv = _dot(vt[hh * HEAD_DIM:(hh + 1) * HEAD_DIM, :], a.astype(BF16))
                acc_ref[hh, :, cols] += jnp.exp(-runs[hh][:, cols]) * pv
                run = runs[hh][:, cols] + suffix[0:1, :]
                if cols != slice(None):
                    run = jnp.concatenate([runs[hh][:, :cols.start], run], axis=1)
                out.append(run)
            return tuple(out)

        acc_ref[...] = jnp.zeros_like(acc_ref)
        runs = (jnp.zeros((1, TQ), F32),) * 2
        n = len(chunks)
        produce(chunks[0], zbufs[0])
        produce(chunks[1], zbufs[1])
        sum_stage(chunks[0], zbufs[0], sbufs[0])
        for i in range(n):
            if i + 2 < n:
                produce(chunks[i + 2], zbufs[(i + 2) % 3])
            if i + 1 < n:
                sum_stage(chunks[i + 1], zbufs[(i + 1) % 3], sbufs[(i + 1) % 2])
            runs = weight_stage(chunks[i], zbufs[i % 3], sbufs[i % 2], runs)
        out_t = jnp.concatenate([acc_ref[0], acc_ref[1]], axis=0)
        o_ref[qi * TQ:(qi + 1) * TQ, :] = jnp.transpose(out_t).astype(BF16)


def _sb_attention(sqt, sbk, svt, batch, seq):
    n = sbk.shape[0]
    npair = W_SB // PAIR
    tspec = lambda w: pl.BlockSpec((None, None, seq // w, LANES, w), lambda b, p: (b, p, 0, 0, 0))
    return pl.pallas_call(
        functools.partial(_sb_kernel, seq=seq),
        grid=(batch, npair),
        in_specs=[tspec(TQ), pl.BlockSpec((seq, PAIR), lambda b, p: (b, p)), tspec(TK)],
        out_specs=pl.BlockSpec((seq, PAIR), lambda b, p: (b, p)),
        out_shape=jax.ShapeDtypeStruct((n, W_SB), BF16),
        scratch_shapes=_attn_scratch(heads=2, score_buffers=5),
        compiler_params=_params(("arbitrary", "arbitrary")),
        name="sb_attn",
    )(sqt, sbk, svt)


def _dil_kernel(q_ref, k_ref, v_ref, o_ref, lse_ref, sa_ref, sb_ref, *, n, dil):
    lo = _lane_lo()
    keeps = (lo, jnp.logical_not(lo))
    r = lax.broadcasted_iota(jnp.int32, (DQ, 2 * DQ), 0)
    c = lax.broadcasted_iota(jnp.int32, (DQ, 2 * DQ), 1)
    band = (c >= r) & (c <= r + DQ)
    causal = (lax.broadcasted_iota(jnp.int32, (DQ, DQ), 1)
              <= lax.broadcasted_iota(jnp.int32, (DQ, DQ), 0))
    zero = jnp.zeros((), BF16)
    npair = W_DIL // PAIR

    def key_rows(blk):
        return slice(0, DQ) if blk == 0 else slice((blk - 1) * DQ, (blk + 1) * DQ)

    def produce(unit, buf):
        stream, blk = unit
        rows, krows = slice(blk * DQ, (blk + 1) * DQ), key_rows(blk)
        for pp in range(npair):
            cols = slice(stream * W_DIL + pp * PAIR, stream * W_DIL + (pp + 1) * PAIR)
            q, k = q_ref[rows, cols], k_ref[krows, cols]
            for hh, keep in enumerate(keeps):
                buf[2 * pp + hh, :, 0:krows.stop - krows.start] = _dot_nt(jnp.where(keep, q, zero), k)

    def consume(unit, buf):
        stream, blk = unit
        rows, krows = slice(blk * DQ, (blk + 1) * DQ), key_rows(blk)
        nk = krows.stop - krows.start
        mask = causal if blk == 0 else band
        for pp in range(npair):
            cols = slice(stream * W_DIL + pp * PAIR, stream * W_DIL + (pp + 1) * PAIR)
            v = v_ref[krows, cols]
            outs, lses = [], []
            for hh, keep in enumerate(keeps):
                s = jnp.where(mask, buf[2 * pp + hh, :, 0:nk], NEG)
                m = jnp.max(s, axis=-1, keepdims=True)
                p = jnp.exp(s - m)
                den = jnp.sum(p, axis=-1, keepdims=True)
                o = _dot(p.astype(BF16), jnp.where(keep, v, zero))
                outs.append(o * (1.0 / den))
                lses.append(m + jnp.log(den))
            o_ref[rows, cols] = (outs[0] + outs[1]).astype(BF16)
            lse_ref[rows, cols] = jnp.where(lo, lses[0], lses[1])

    units = [(stream, blk) for stream in range(dil) for blk in range(n // DQ)]
    bufs = (sa_ref, sb_ref)
    produce(units[0], bufs[0])
    for u, unit in enumerate(units):
        if u + 1 < len(units):
            produce(units[u + 1], bufs[(u + 1) % 2])
        consume(unit, bufs[u % 2])


def _dil_attention(dq, dk, dv, group, batch, seq):
    window, dil = DIL_PATTERNS[group]
    assert window // dil == DQ
    n = seq // dil
    spec = pl.BlockSpec((None, n, dil * W_DIL), lambda b: (b, 0, 0))
    return pl.pallas_call(
        functools.partial(_dil_kernel, n=n, dil=dil),
        grid=(batch,),
        in_specs=[spec, spec, spec],
        out_specs=[spec, spec],
        out_shape=[jax.ShapeDtypeStruct((batch, n, dil * W_DIL), BF16),
                   jax.ShapeDtypeStruct((batch, n, dil * W_DIL), F32)],
        scratch_shapes=[pltpu.VMEM((N_HEADS_DIL, DQ, 2 * DQ), F32)] * 2,
        compiler_params=_params(("arbitrary",)),
        name=f"dil_attn_g{group}",
    )(dq, dk, dv)


def _merge_kernel(x_ref, oa_ref, ob_ref, o0_ref, o1_ref, o2_ref, l0_ref, l1_ref, l2_ref, gate_ref,
                  wa_ref, wb_ref, wc_ref, wo_ref, out_ref, stage_ref):
    def tokens(ref, g):
        dil = DIL_PATTERNS[g][1]
        if dil == 1:
            return ref[...].astype(F32)
        npair = W_DIL // PAIR
        for s in range(dil):
            for pp in range(npair):
                col = s * W_DIL + pp * PAIR
                stage_ref[g * npair + pp, pl.ds(s, TM // dil, stride=dil), :] = ref[:, col:col + PAIR].astype(F32)
        return jnp.concatenate([stage_ref[g * npair + pp] for pp in range(npair)], axis=1)

    lses = [tokens(ref, g) for g, ref in enumerate((l0_ref, l1_ref, l2_ref))]
    mx = jnp.maximum(jnp.maximum(lses[0], lses[1]), lses[2])
    es = [jnp.exp(l - mx) for l in lses]
    inv = 1.0 / (es[0] + es[1] + es[2])
    out_c = None
    for g, ref in enumerate((o0_ref, o1_ref, o2_ref)):
        term = (es[g] * inv) * tokens(ref, g)
        out_c = term if out_c is None else out_c + term
    ys = (_dot(oa_ref[...], wa_ref[...]), _dot(ob_ref[...], wb_ref[...]), _dot(out_c.astype(BF16), wc_ref[...]))
    merged = None
    for b in range(N_BRANCHES):
        gate = 1.0 / (1.0 + jnp.exp(-gate_ref[:, b * D_MODEL:(b + 1) * D_MODEL].astype(F32)))
        merged = gate * ys[b] if merged is None else merged + gate * ys[b]
    out_ref[...] = x_ref[...] + _dot(merged.astype(BF16), wo_ref[...])


def _merge(x2, out_a, out_b, os_, lses, gates, wa, wb, wc, wo, seq):
    n = x2.shape[0]
    nt = seq // TM
    row = lambda i: (i, 0)
    tile = lambda w: pl.BlockSpec((TM, w), row)
    streams = [pl.BlockSpec((None, TM // d, d * W_DIL), lambda i: (i // nt, i % nt, 0)) for _, d in DIL_PATTERNS]
    return pl.pallas_call(
        _merge_kernel,
        grid=(n // TM,),
        in_specs=[tile(D_MODEL), tile(W_FOX), tile(W_SB)] + streams * 2 + [tile(SZ_GATES)]
                 + [_resident(w.shape) for w in (wa, wb, wc, wo)],
        out_specs=tile(D_MODEL),
        out_shape=jax.ShapeDtypeStruct((n, D_MODEL), F32),
        scratch_shapes=[pltpu.VMEM((W_DIL_ALL // PAIR, TM, LANES), F32)],
        compiler_params=_params(("arbitrary",)),
        name="merge",
    )(x2, out_a, out_b, *os_, *lses, gates, wa, wb, wc, wo)


def _mlp_kernel(x_ref, g_ref, w1_ref, w2_ref, out_ref):
    x = x_ref[...]
    ms = jnp.mean(x * x, axis=-1, keepdims=True)
    h = (x * lax.rsqrt(ms + EPS) * g_ref[...]).astype(BF16)
    acc = x
    for c in range(0, D_FF, FF_CHUNK):
        u = jnp.maximum(_dot(h, w1_ref[:, c:c + FF_CHUNK]), 0.0)
        acc = acc + _dot((u * u).astype(BF16), w2_ref[c:c + FF_CHUNK, :])
    out_ref[...] = acc


def _mlp(x2, g, w1, w2):
    n = x2.shape[0]
    row = lambda i: (i, 0)
    return pl.pallas_call(
        _mlp_kernel,
        grid=(n // TM,),
        in_specs=[pl.BlockSpec((TM, D_MODEL), row), _resident((1, D_MODEL)),
                  _resident(w1.shape), _resident(w2.shape)],
        out_specs=pl.BlockSpec((TM, D_MODEL), row),
        out_shape=jax.ShapeDtypeStruct((n, D_MODEL), F32),
        compiler_params=_params(("arbitrary",)),
        name="mlp",
    )(x2, g, w1, w2)


def _prep_tables():
    half = HEAD_DIM // 2
    lane = np.arange(LANES)
    inv = (1.0 / (ROPE_THETA ** (np.arange(half, dtype=np.float32) / half))).astype(np.float32)
    first = (lane % HEAD_DIM) < half
    rope_tab = jnp.asarray(np.stack([np.where(first, -1.0, 1.0), first.astype(np.float32)]), F32)
    invf = jnp.asarray(np.repeat(inv[:, None], TM, axis=1), F32)
    tri = jnp.asarray(np.tril(np.ones((TM, TM), np.float32)), BF16)
    return (rope_tab, invf, tri) + _aug_tables()


def kernel(x, positions, attn_norm, w_in, b_forget, q_norm_fox, k_norm_fox, q_norm_dil, k_norm_dil,
           w_up_fox, w_up_sb, w_up_dil, w_out, mlp_norm, w_mlp_in, w_mlp_out):
    batch, seq, _ = x.shape
    depth = w_in.shape[0]
    n = batch * seq
    o1 = SZ_FOX_QKV
    o2 = o1 + SZ_FORGET
    w_front = (w_in[:, :, :o1].astype(BF16), w_in[:, :, o2:].astype(BF16),
               jnp.pad(w_in[:, :, o1:o2], ((0, 0), (0, 0), (0, LANES - SZ_FORGET))).astype(BF16))
    wa, wb, wc, wo = (w.astype(BF16) for w in (w_up_fox, w_up_sb, w_up_dil, w_out))
    w1, w2 = w_mlp_in.astype(BF16), w_mlp_out.astype(BF16)
    pair = lambda g: jnp.tile(g.astype(F32), (1, LANES // HEAD_DIM))
    gains = jnp.stack([pair(q_norm_fox) * (SCALE * LOG2E), pair(k_norm_fox),
                       pair(q_norm_dil) * SCALE, pair(k_norm_dil)], axis=1)
    bf = jnp.pad(b_forget.astype(F32), ((0, 0), (0, LANES - SZ_FORGET)))[:, None, :]
    tables = _prep_tables()
    pos = positions.reshape(n // TM, 1, TM)

    x2 = x.reshape(n, D_MODEL)
    for l in range(depth):
        front = _front(x2, attn_norm[l][None, :], [w[l] for w in w_front], pos, gains[l], bf[l], tables, batch, seq)
        fqt, fk, fvt, sqt, sbk, svt = front[:6]
        dqs, dks, dvs, gates = front[6:9], front[9:12], front[12:15], front[15]
        out_a = _fox_attention(fqt, fk, fvt, batch, seq)
        out_b = _sb_attention(sqt, sbk, svt, batch, seq)
        dil = [_dil_attention(dqs[g], dks[g], dvs[g], g, batch, seq) for g in range(N_DIL_GROUPS)]
        x2 = _merge(x2, out_a, out_b, [d[0] for d in dil], [d[1] for d in dil], gates,
                    wa[l], wb[l], wc[l], wo[l], seq)
        x2 = _mlp(x2, mlp_norm[l][None, :], w1[l], w2[l])
    return x2.reshape(batch, seq, D_MODEL)
```

```python
import functools
import math

import numpy as np
import jax
import jax.numpy as jnp
from jax import lax
from jax.experimental import pallas as pl
from jax.experimental.pallas import tpu as pltpu

D_MODEL = 1024
HEAD_DIM = 64
N_HEADS_FOX = 8
N_HEADS_SB = 8
DIL_PATTERNS = ((128, 1), (512, 4), (2048, 16))
N_DIL_GROUPS = len(DIL_PATTERNS)
N_HEADS_DIL = 4
D_FF = 4 * D_MODEL
ROPE_THETA = 10000.0
EPS = 1e-6
N_BRANCHES = 3

W_FOX = N_HEADS_FOX * HEAD_DIM
W_SB = N_HEADS_SB * HEAD_DIM
W_DIL = N_HEADS_DIL * HEAD_DIM
W_DIL_ALL = N_DIL_GROUPS * W_DIL
SZ_FOX_QKV = 3 * W_FOX
SZ_FORGET = N_HEADS_FOX
SZ_SB_QKV = 3 * W_SB
SZ_DIL_QKV = 3 * W_DIL_ALL
SZ_GATES = N_BRANCHES * D_MODEL

LANES = 128
PAIR = LANES
SCALE = 1.0 / math.sqrt(HEAD_DIM)
LOG2E = math.log2(math.e)
NEG = -1e30

TM = 512
TQ = 512
TK = 256
FOX_HEADS_PER_STEP = 4
DQ = 128
PROJ_CHUNK = 768
FF_CHUNK = 1024
VMEM_LIMIT = 56 * 1024 * 1024

AUG_F = 0
AUG_ONE = 3

F32 = jnp.float32
BF16 = jnp.bfloat16


def _params(sem):
    return pltpu.CompilerParams(dimension_semantics=sem, vmem_limit_bytes=VMEM_LIMIT)


def _resident(shape):
    nd = len(shape)
    return pl.BlockSpec(shape, lambda *_: (0,) * nd, pipeline_mode=pl.Buffered(1))


def _dot(a, b):
    return jnp.dot(a, b, preferred_element_type=F32)


def _dot_nt(a, b):
    return lax.dot_general(a, b, (((1,), (1,)), ((), ())), preferred_element_type=F32)


def _split3(x):
    hi = x.astype(BF16)
    r1 = x - hi.astype(F32)
    mid = r1.astype(BF16)
    lo = (r1 - mid.astype(F32)).astype(BF16)
    return hi, mid, lo


def _lane_lo():
    return lax.broadcasted_iota(jnp.int32, (1, LANES), 1) < HEAD_DIM


OFF_FOX = 0
OFF_SB = OFF_FOX + SZ_FOX_QKV
OFF_DIL = OFF_SB + SZ_SB_QKV
OFF_GATES = OFF_DIL + SZ_DIL_QKV
OFF_FORGET = OFF_GATES + SZ_GATES
MXU_COLS = 256


def _head_rms(y, head_sum):
    sq = (y * y).astype(BF16)
    ss = jnp.concatenate([_dot(sq[:, c:c + MXU_COLS], head_sum) for c in range(0, y.shape[1], MXU_COLS)], axis=1)
    return y * lax.rsqrt(ss * (1.0 / HEAD_DIM) + EPS)


def _put_transposed(out_ref, idx, tile, width):
    tt = jnp.transpose(tile).astype(BF16)
    for i in range(TM // width):
        out_ref[idx, i] = tt[:, i * width:(i + 1) * width]


def _front_kernel(x_ref, g_ref, w_fox_ref, w_rest_ref, w_forget_ref, pos_ref, gains_ref, bf_ref,
                  rope_ref, invf_ref, tri_ref, eaug_ref, ones_ref,
                  fqt_out, fk_out, fvt_out, sqt_out, sbk_out, svt_out,
                  dq0, dq1, dq2, dk0, dk1, dk2, dv0, dv1, dv2, gate_out, carry_ref, stage_ref):
    t = pl.program_id(1)
    lo = _lane_lo()
    gq_fox, gk_fox, gq_dil, gk_dil = (gains_ref[i:i + 1, :] for i in range(4))

    x = x_ref[...]
    ms = jnp.mean(x * x, axis=-1, keepdims=True)
    h = (x * lax.rsqrt(ms + EPS) * g_ref[...]).astype(BF16)

    def proj(col, width):
        if col >= OFF_FORGET:
            ref, col = w_forget_ref, col - OFF_FORGET
        elif col >= OFF_SB:
            ref, col = w_rest_ref, col - OFF_SB
        else:
            ref = w_fox_ref
        return _dot(h, ref[:, col:col + width])

    gate_cols = iter(range(0, SZ_GATES, MXU_COLS))

    def gate_step():
        c = next(gate_cols, None)
        if c is not None:
            gate_out[:, c:c + MXU_COLS] = proj(OFF_GATES + c, MXU_COLS).astype(BF16)
        return c is not None

    @pl.when(t == 0)
    def _():
        carry_ref[...] = jnp.zeros_like(carry_ref)

    z = proj(OFF_FORGET, LANES) + bf_ref[...]
    log_f = jnp.minimum(z, 0.0) - jnp.log(1.0 + jnp.exp(-jnp.abs(z)))
    tri = tri_ref[...]
    hi, mid, lw = _split3(log_f)
    cum3 = _dot(tri, jnp.concatenate([hi, mid, lw], axis=1))
    f_cum = (cum3[:, :LANES] + cum3[:, LANES:2 * LANES] + cum3[:, 2 * LANES:]) + carry_ref[0:1, :]
    carry_ref[...] = jnp.broadcast_to(f_cum[TM - 1:TM, :], carry_ref.shape)
    hi, mid, lw = _split3(f_cum * LOG2E)
    is_head = lax.broadcasted_iota(jnp.int32, (1, LANES), 1) < N_HEADS_FOX
    mid_lo = (jnp.where(is_head, mid.astype(F32), 0.0)
              + pltpu.roll(jnp.where(is_head, lw.astype(F32), 0.0), N_HEADS_FOX, 1)).astype(BF16)
    aug = _dot(jnp.concatenate([hi, mid_lo], axis=1), eaug_ref[...]) + ones_ref[...]

    hr = lax.broadcasted_iota(jnp.int32, (MXU_COLS, MXU_COLS), 0) // HEAD_DIM
    hc = lax.broadcasted_iota(jnp.int32, (MXU_COLS, MXU_COLS), 1) // HEAD_DIM
    head_sum = jnp.where(hr == hc, 1.0, 0.0).astype(BF16)

    fox_q = _head_rms(proj(OFF_FOX, W_FOX), head_sum)
    fox_k = _head_rms(proj(OFF_FOX + W_FOX, W_FOX), head_sum)
    for p in range(W_FOX // PAIR):
        sl = slice(p * PAIR, (p + 1) * PAIR)
        qn = fox_q[:, sl] * gq_fox
        kn = fox_k[:, sl] * gk_fox
        for hh in range(2):
            hd = 2 * p + hh
            own = lo if hh == 0 else jnp.logical_not(lo)
            _put_transposed(fqt_out, hd, jnp.where(own, qn, aug[:, hd * LANES:(hd + 1) * LANES]), TQ)
            ka = aug[:, (N_HEADS_FOX + hd) * LANES:(N_HEADS_FOX + hd + 1) * LANES]
            fk_out[:, hd * LANES:(hd + 1) * LANES] = jnp.where(own, kn, ka).astype(BF16)
        gate_step()
    for out, col, width, mult in ((fvt_out, OFF_FOX + 2 * W_FOX, TK, None), (sqt_out, OFF_SB, TQ, SCALE),
                                  (svt_out, OFF_SB + 2 * W_SB, TK, None)):
        y = proj(col, W_SB)
        for p in range(W_SB // PAIR):
            tile = y[:, p * PAIR:(p + 1) * PAIR]
            _put_transposed(out, p, tile if mult is None else tile * mult, width)
        gate_step()
    sbk_out[...] = proj(OFF_SB + W_SB, W_SB).astype(BF16)

    ang_t = pos_ref[...].astype(F32) * invf_ref[...]
    reps = LANES // (HEAD_DIM // 2)
    cos = jnp.transpose(jnp.concatenate([jnp.cos(ang_t)] * reps, axis=0))
    sin_signed = jnp.transpose(jnp.concatenate([jnp.sin(ang_t)] * reps, axis=0)) * rope_ref[0:1, :]
    first_half = rope_ref[1:2, :] > 0.5

    def rope(x):
        partner = jnp.where(first_half, pltpu.roll(x, LANES - HEAD_DIM // 2, 1), pltpu.roll(x, HEAD_DIM // 2, 1))
        return x * cos + partner * sin_signed

    npair_dil = W_DIL // PAIR

    def put_streams(outs):
        for g, (_, dil) in enumerate(DIL_PATTERNS):
            for s in range(dil):
                rows = pl.ds(s, TM // dil, stride=dil) if dil > 1 else slice(None)
                for pp in range(npair_dil):
                    col = s * W_DIL + pp * PAIR
                    outs[g][:, col:col + PAIR] = stage_ref[g * npair_dil + pp, rows, :].astype(BF16)

    for i, (gain, outs) in enumerate(((gq_dil, (dq0, dq1, dq2)), (gk_dil, (dk0, dk1, dk2)), (None, (dv0, dv1, dv2)))):
        y = proj(OFF_DIL + i * W_DIL_ALL, W_DIL_ALL)
        if gain is not None:
            y = _head_rms(y, head_sum)
        for p in range(W_DIL_ALL // PAIR):
            tile = y[:, p * PAIR:(p + 1) * PAIR]
            stage_ref[p] = tile if gain is None else rope(tile * gain)
            if p % 3 == 2:
                gate_step()
        put_streams(outs)
    while gate_step():
        pass


def _aug_tables():
    e = np.zeros((2 * LANES, 2 * N_HEADS_FOX * LANES), np.float32)
    ones = np.zeros((1, 2 * N_HEADS_FOX * LANES), np.float32)
    for h in range(N_HEADS_FOX):
        other = HEAD_DIM if h % 2 == 0 else 0
        qbase, kbase = h * LANES + other, (N_HEADS_FOX + h) * LANES + other
        for part in range(3):
            row = h if part == 0 else LANES + (part - 1) * N_HEADS_FOX + h
            e[row, qbase + AUG_F + part] = 1.0
            ones[0, kbase + AUG_F + part] = 1.0
            e[row, kbase + AUG_ONE + part] = -1.0
            ones[0, qbase + AUG_ONE + part] = 1.0
    return jnp.asarray(e, BF16), jnp.asarray(ones, F32)


def _front(x2, g, weights, pos, gains, bf, tables, batch, seq):
    n = x2.shape[0]
    nt = seq // TM
    row = lambda b, t: (b * nt + t, 0)
    tchunk = lambda heads, w: pl.BlockSpec((None, heads, TM // w, LANES, w), lambda b, t: (b, 0, t, 0, 0))
    tshape = lambda heads, w: jax.ShapeDtypeStruct((batch, heads, seq // w, LANES, w), BF16)
    stream_specs = [pl.BlockSpec((None, TM // d, d * W_DIL), lambda b, t: (b, t, 0)) for _, d in DIL_PATTERNS] * 3
    stream_shapes = [jax.ShapeDtypeStruct((batch, seq // d, d * W_DIL), BF16) for _, d in DIL_PATTERNS] * 3
    npair = W_FOX // PAIR
    return pl.pallas_call(
        _front_kernel,
        grid=(batch, nt),
        in_specs=[pl.BlockSpec((TM, D_MODEL), row), _resident((1, D_MODEL))]
                 + [_resident(w.shape) for w in weights]
                 + [pl.BlockSpec((None, 1, TM), lambda b, t: (b * nt + t, 0, 0)),
                  _resident((4, LANES)), _resident((1, LANES))]
                 + [_resident(a.shape) for a in tables],
        out_specs=[tchunk(N_HEADS_FOX, TQ), pl.BlockSpec((TM, N_HEADS_FOX * LANES), row), tchunk(npair, TK),
                   tchunk(npair, TQ), pl.BlockSpec((TM, W_SB), row), tchunk(npair, TK)] + stream_specs
                  + [pl.BlockSpec((TM, SZ_GATES), row)],
        out_shape=[tshape(N_HEADS_FOX, TQ), jax.ShapeDtypeStruct((n, N_HEADS_FOX * LANES), BF16), tshape(npair, TK),
                   tshape(npair, TQ), jax.ShapeDtypeStruct((n, W_SB), BF16), tshape(npair, TK)] + stream_shapes
                  + [jax.ShapeDtypeStruct((n, SZ_GATES), BF16)],
        scratch_shapes=[pltpu.VMEM((8, LANES), F32), pltpu.VMEM((W_DIL_ALL // PAIR, TM, LANES), F32)],
        compiler_params=_params(("arbitrary", "arbitrary")),
        name="front",
    )(x2, g, *weights, pos, gains, bf, *tables)


def _chunk_rows(j):
    return pl.ds(j * TK if isinstance(j, int) else pl.multiple_of(j * TK, TK), TK)


def _attn_scratch(heads, score_buffers):
    return [pltpu.VMEM((heads, TK, TQ), F32)] * score_buffers + [pltpu.VMEM((heads, HEAD_DIM, TQ), F32)]


def _fox_kernel(qt_ref, k_ref, vt_ref, o_ref, sa_ref, sb_ref, acc_ref, *, seq):
    assert TQ == 2 * TK
    r = lax.broadcasted_iota(jnp.int32, (TK, TQ), 0)
    c = lax.broadcasted_iota(jnp.int32, (TK, TQ), 1)
    causal = r <= c
    causal_late = (lax.broadcasted_iota(jnp.int32, (TK, TK), 0)
                   <= lax.broadcasted_iota(jnp.int32, (TK, TK), 1))
    late = slice(TK, TQ)

    for qi in range(seq // TQ):
        qt = [qt_ref[hh, qi] for hh in range(FOX_HEADS_PER_STEP)]

        def produce(j, buf, cols=slice(None)):
            for hh in range(FOX_HEADS_PER_STEP):
                buf[hh, :, cols] = _dot(k_ref[_chunk_rows(j), hh * LANES:(hh + 1) * LANES], qt[hh][:, cols])

        def consume(j, buf, state, mask, cols=slice(None)):
            out = []
            for hh in range(FOX_HEADS_PER_STEP):
                vt = vt_ref[hh // 2, j]
                m_all, l_all = state[2 * hh:2 * hh + 2]
                m_prev, l_prev = m_all[:, cols], l_all[:, cols]
                s = buf[hh, :, cols]
                if mask is not None:
                    s = jnp.where(mask, s, NEG)
                m_new = jnp.maximum(m_prev, jnp.max(s, axis=0, keepdims=True))
                alpha = jnp.exp2(m_prev - m_new)
                p = jnp.exp2(s - m_new)
                l_new = alpha * l_prev + jnp.sum(p, axis=0, keepdims=True)
                pv = _dot(vt[(hh % 2) * HEAD_DIM:(hh % 2 + 1) * HEAD_DIM, :], p.astype(BF16))
                acc_ref[hh, :, cols] = alpha * acc_ref[hh, :, cols] + pv
                if cols != slice(None):
                    m_new = jnp.concatenate([m_all[:, :cols.start], m_new], axis=1)
                    l_new = jnp.concatenate([l_all[:, :cols.start], l_new], axis=1)
                out += [m_new, l_new]
            return tuple(out)

        n_full = qi * (TQ // TK)
        acc_ref[...] = jnp.zeros_like(acc_ref)
        init = (jnp.full((1, TQ), NEG, F32), jnp.zeros((1, TQ), F32))
        st = init * FOX_HEADS_PER_STEP
        produce(0, sa_ref)
        for i in range(n_full // 2):
            j = 2 * i
            produce(j + 1, sb_ref)
            st = consume(j, sa_ref, st, None)
            produce(j + 2, sa_ref)
            st = consume(j + 1, sb_ref, st, None)
        produce(n_full + 1, sb_ref, late)
        st = consume(n_full, sa_ref, st, causal)
        st = consume(n_full + 1, sb_ref, st, causal_late, late)
        for pp in range(FOX_HEADS_PER_STEP // 2):
            h0, h1 = 2 * pp, 2 * pp + 1
            out_t = jnp.concatenate([acc_ref[h0] * (1.0 / st[2 * h0 + 1]), acc_ref[h1] * (1.0 / st[2 * h1 + 1])],
                                    axis=0)
            o_ref[qi * TQ:(qi + 1) * TQ, pp * PAIR:(pp + 1) * PAIR] = jnp.transpose(out_t).astype(BF16)


def _fox_attention(fqt, fk, fvt, batch, seq):
    n = fk.shape[0]
    hps = FOX_HEADS_PER_STEP
    blk = lambda b, g: (b, g, 0, 0, 0)
    return pl.pallas_call(
        functools.partial(_fox_kernel, seq=seq),
        grid=(batch, N_HEADS_FOX // hps),
        in_specs=[pl.BlockSpec((None, hps, seq // TQ, LANES, TQ), blk),
                  pl.BlockSpec((seq, hps * LANES), lambda b, g: (b, g)),
                  pl.BlockSpec((None, hps // 2, seq // TK, LANES, TK), blk)],
        out_specs=pl.BlockSpec((seq, hps * HEAD_DIM), lambda b, g: (b, g)),
        out_shape=jax.ShapeDtypeStruct((n, W_FOX), BF16),
        scratch_shapes=_attn_scratch(heads=hps, score_buffers=2),
        compiler_params=_params(("arbitrary", "arbitrary")),
        name="fox_attn",
    )(fqt, fk, fvt)


def _sb_kernel(qt_ref, k_ref, vt_ref, o_ref, z0_ref, z1_ref, z2_ref, s0_ref, s1_ref, acc_ref, *, seq):
    assert TQ == 2 * TK
    lo = _lane_lo()
    r = lax.broadcasted_iota(jnp.int32, (TK, TQ), 0)
    c = lax.broadcasted_iota(jnp.int32, (TK, TQ), 1)
    strict = r < c
    rk = lax.broadcasted_iota(jnp.int32, (TK, TK), 0)
    ck = lax.broadcasted_iota(jnp.int32, (TK, TK), 1)
    strict_late = rk < ck
    late = slice(TK, TQ)
    tri = jnp.where(rk <= ck, 1.0, 0.0).astype(BF16)
    zero = jnp.zeros((), BF16)

    zbufs, sbufs = (z0_ref, z1_ref, z2_ref), (s0_ref, s1_ref)

    for qi in range(seq // TQ):
        qt = qt_ref[qi]
        n_full = qi * (TQ // TK)
        chunks = ([(n_full + 1, strict_late, late), (n_full, strict, slice(None))]
                  + [(j, None, slice(None)) for j in reversed(range(n_full))])

        def produce(chunk, zb):
            j, _, cols = chunk
            kc = k_ref[j * TK:(j + 1) * TK, :]
            zb[0, :, cols] = _dot(jnp.where(lo, kc, zero), qt[:, cols])
            zb[1, :, cols] = _dot(jnp.where(lo, zero, kc), qt[:, cols])

        def sum_stage(chunk, zb, sb):
            _, mask, cols = chunk
            for hh in range(2):
                z = zb[hh, :, cols].astype(BF16)
                sp = jnp.maximum(z, 0) + jnp.log(1 + jnp.exp(-jnp.abs(z)))
                if mask is not None:
                    sp = jnp.where(mask, sp, jnp.zeros((), BF16))
                sb[hh, :, cols] = _dot(tri, sp)

        def weight_stage(chunk, zb, sb, runs):
            j, mask, cols = chunk
            vt = vt_ref[j]
            out = []
            for hh in range(2):
                suffix = sb[hh, :, cols]
                a = jnp.exp(zb[hh, :, cols] - suffix)
                if mask is not None:
                    a = jnp.where(mask, a, 0.0)
                pv = _dot(vt[hh * HEAD_DIM:(hh + 1) * HEAD_DIM, :], a.astype(BF16))
                acc_ref[hh, :, cols] += jnp.exp(-runs[hh][:, cols]) * pv
                run = runs[hh][:, cols] + suffix[0:1, :]
                if cols != slice(None):
                    run = jnp.concatenate([runs[hh][:, :cols.start], run], axis=1)
                out.append(run)
            return tuple(out)

        acc_ref[...] = jnp.zeros_like(acc_ref)
        runs = (jnp.zeros((1, TQ), F32),) * 2
        n = len(chunks)
        produce(chunks[0], zbufs[0])
        produce(chunks[1], zbufs[1])
        sum_stage(chunks[0], zbufs[0], sbufs[0])
        for i in range(n):
            if i + 2 < n:
                produce(chunks[i + 2], zbufs[(i + 2) % 3])
            if i + 1 < n:
                sum_stage(chunks[i + 1], zbufs[(i + 1) % 3], sbufs[(i + 1) % 2])
            runs = weight_stage(chunks[i], zbufs[i % 3], sbufs[i % 2], runs)
        out_t = jnp.concatenate([acc_ref[0], acc_ref[1]], axis=0)
        o_ref[qi * TQ:(qi + 1) * TQ, :] = jnp.transpose(out_t).astype(BF16)


def _sb_attention(sqt, sbk, svt, batch, seq):
    n = sbk.shape[0]
    npair = W_SB // PAIR
    tspec = lambda w: pl.BlockSpec((None, None, seq // w, LANES, w), lambda b, p: (b, p, 0, 0, 0))
    return pl.pallas_call(
        functools.partial(_sb_kernel, seq=seq),
        grid=(batch, npair),
        in_specs=[tspec(TQ), pl.BlockSpec((seq, PAIR), lambda b, p: (b, p)), tspec(TK)],
        out_specs=pl.BlockSpec((seq, PAIR), lambda b, p: (b, p)),
        out_shape=jax.ShapeDtypeStruct((n, W_SB), BF16),
        scratch_shapes=_attn_scratch(heads=2, score_buffers=5),
        compiler_params=_params(("arbitrary", "arbitrary")),
        name="sb_attn",
    )(sqt, sbk, svt)


def _dil_kernel(q_ref, k_ref, v_ref, o_ref, lse_ref, sa_ref, sb_ref, *, n, dil):
    lo = _lane_lo()
    keeps = (lo, jnp.logical_not(lo))
    r = lax.broadcasted_iota(jnp.int32, (DQ, 2 * DQ), 0)
    c = lax.broadcasted_iota(jnp.int32, (DQ, 2 * DQ), 1)
    band = (c >= r) & (c <= r + DQ)
    causal = (lax.broadcasted_iota(jnp.int32, (DQ, DQ), 1)
              <= lax.broadcasted_iota(jnp.int32, (DQ, DQ), 0))
    zero = jnp.zeros((), BF16)
    npair = W_DIL // PAIR

    def key_rows(blk):
        return slice(0, DQ) if blk == 0 else slice((blk - 1) * DQ, (blk + 1) * DQ)

    def produce(unit, buf):
        stream, blk = unit
        rows, krows = slice(blk * DQ, (blk + 1) * DQ), key_rows(blk)
        for pp in range(npair):
            cols = slice(stream * W_DIL + pp * PAIR, stream * W_DIL + (pp + 1) * PAIR)
            q, k = q_ref[rows, cols], k_ref[krows, cols]
            for hh, keep in enumerate(keeps):
                buf[2 * pp + hh, :, 0:krows.stop - krows.start] = _dot_nt(jnp.where(keep, q, zero), k)

    def consume(unit, buf):
        stream, blk = unit
        rows, krows = slice(blk * DQ, (blk + 1) * DQ), key_rows(blk)
        nk = krows.stop - krows.start
        mask = causal if blk == 0 else band
        for pp in range(npair):
            cols = slice(stream * W_DIL + pp * PAIR, stream * W_DIL + (pp + 1) * PAIR)
            v = v_ref[krows, cols]
            outs, lses = [], []
            for hh, keep in enumerate(keeps):
                s = jnp.where(mask, buf[2 * pp + hh, :, 0:nk], NEG)
                m = jnp.max(s, axis=-1, keepdims=True)
                p = jnp.exp(s - m)
                den = jnp.sum(p, axis=-1, keepdims=True)
                o = _dot(p.astype(BF16), jnp.where(keep, v, zero))
                outs.append(o * (1.0 / den))
                lses.append(m + jnp.log(den))
            o_ref[rows, cols] = (outs[0] + outs[1]).astype(BF16)
            lse_ref[rows, cols] = jnp.where(lo, lses[0], lses[1])

    units = [(stream, blk) for stream in range(dil) for blk in range(n // DQ)]
    bufs = (sa_ref, sb_ref)
    produce(units[0], bufs[0])
    for u, unit in enumerate(units):
        if u + 1 < len(units):
            produce(units[u + 1], bufs[(u + 1) % 2])
        consume(unit, bufs[u % 2])


def _dil_attention(dq, dk, dv, group, batch, seq):
    window, dil = DIL_PATTERNS[group]
    assert window // dil == DQ
    n = seq // dil
    spec = pl.BlockSpec((None, n, dil * W_DIL), lambda b: (b, 0, 0))
    return pl.pallas_call(
        functools.partial(_dil_kernel, n=n, dil=dil),
        grid=(batch,),
        in_specs=[spec, spec, spec],
        out_specs=[spec, spec],
        out_shape=[jax.ShapeDtypeStruct((batch, n, dil * W_DIL), BF16),
                   jax.ShapeDtypeStruct((batch, n, dil * W_DIL), F32)],
        scratch_shapes=[pltpu.VMEM((N_HEADS_DIL, DQ, 2 * DQ), F32)] * 2,
        compiler_params=_params(("arbitrary",)),
        name=f"dil_attn_g{group}",
    )(dq, dk, dv)


def _merge_kernel(x_ref, oa_ref, ob_ref, o0_ref, o1_ref, o2_ref, l0_ref, l1_ref, l2_ref, gate_ref,
                  wa_ref, wb_ref, wc_ref, wo_ref, out_ref, stage_ref):
    def tokens(ref, g):
        dil = DIL_PATTERNS[g][1]
        if dil == 1:
            return ref[...].astype(F32)
        npair = W_DIL // PAIR
        for s in range(dil):
            for pp in range(npair):
                col = s * W_DIL + pp * PAIR
                stage_ref[g * npair + pp, pl.ds(s, TM // dil, stride=dil), :] = ref[:, col:col + PAIR].astype(F32)
        return jnp.concatenate([stage_ref[g * npair + pp] for pp in range(npair)], axis=1)

    lses = [tokens(ref, g) for g, ref in enumerate((l0_ref, l1_ref, l2_ref))]
    mx = jnp.maximum(jnp.maximum(lses[0], lses[1]), lses[2])
    es = [jnp.exp(l - mx) for l in lses]
    inv = 1.0 / (es[0] + es[1] + es[2])
    out_c = None
    for g, ref in enumerate((o0_ref, o1_ref, o2_ref)):
        term = (es[g] * inv) * tokens(ref, g)
        out_c = term if out_c is None else out_c + term
    ys = (_dot(oa_ref[...], wa_ref[...]), _dot(ob_ref[...], wb_ref[...]), _dot(out_c.astype(BF16), wc_ref[...]))
    merged = None
    for b in range(N_BRANCHES):
        gate = 1.0 / (1.0 + jnp.exp(-gate_ref[:, b * D_MODEL:(b + 1) * D_MODEL].astype(F32)))
        merged = gate * ys[b] if merged is None else merged + gate * ys[b]
    out_ref[...] = x_ref[...] + _dot(merged.astype(BF16), wo_ref[...])


def _merge(x2, out_a, out_b, os_, lses, gates, wa, wb, wc, wo, seq):
    n = x2.shape[0]
    nt = seq // TM
    row = lambda i: (i, 0)
    tile = lambda w: pl.BlockSpec((TM, w), row)
    streams = [pl.BlockSpec((None, TM // d, d * W_DIL), lambda i: (i // nt, i % nt, 0)) for _, d in DIL_PATTERNS]
    return pl.pallas_call(
        _merge_kernel,
        grid=(n // TM,),
        in_specs=[tile(D_MODEL), tile(W_FOX), tile(W_SB)] + streams * 2 + [tile(SZ_GATES)]
                 + [_resident(w.shape) for w in (wa, wb, wc, wo)],
        out_specs=tile(D_MODEL),
        out_shape=jax.ShapeDtypeStruct((n, D_MODEL), F32),
        scratch_shapes=[pltpu.VMEM((W_DIL_ALL // PAIR, TM, LANES), F32)],
        compiler_params=_params(("arbitrary",)),
        name="merge",
    )(x2, out_a, out_b, *os_, *lses, gates, wa, wb, wc, wo)


def _mlp_kernel(x_ref, g_ref, w1_ref, w2_ref, out_ref):
    x = x_ref[...]
    ms = jnp.mean(x * x, axis=-1, keepdims=True)
    h = (x * lax.rsqrt(ms + EPS) * g_ref[...]).astype(BF16)
    acc = x
    for c in range(0, D_FF, FF_CHUNK):
        u = jnp.maximum(_dot(h, w1_ref[:, c:c + FF_CHUNK]), 0.0)
        acc = acc + _dot((u * u).astype(BF16), w2_ref[c:c + FF_CHUNK, :])
    out_ref[...] = acc


def _mlp(x2, g, w1, w2):
    n = x2.shape[0]
    row = lambda i: (i, 0)
    return pl.pallas_call(
        _mlp_kernel,
        grid=(n // TM,),
        in_specs=[pl.BlockSpec((TM, D_MODEL), row), _resident((1, D_MODEL)),
                  _resident(w1.shape), _resident(w2.shape)],
        out_specs=pl.BlockSpec((TM, D_MODEL), row),
        out_shape=jax.ShapeDtypeStruct((n, D_MODEL), F32),
        compiler_params=_params(("arbitrary",)),
        name="mlp",
    )(x2, g, w1, w2)


def _prep_tables():
    half = HEAD_DIM // 2
    lane = np.arange(LANES)
    inv = (1.0 / (ROPE_THETA ** (np.arange(half, dtype=np.float32) / half))).astype(np.float32)
    first = (lane % HEAD_DIM) < half
    rope_tab = jnp.asarray(np.stack([np.where(first, -1.0, 1.0), first.astype(np.float32)]), F32)
    invf = jnp.asarray(np.repeat(inv[:, None], TM, axis=1), F32)
    tri = jnp.asarray(np.tril(np.ones((TM, TM), np.float32)), BF16)
    return (rope_tab, invf, tri) + _aug_tables()


def kernel(x, positions, attn_norm, w_in, b_forget, q_norm_fox, k_norm_fox, q_norm_dil, k_norm_dil,
           w_up_fox, w_up_sb, w_up_dil, w_out, mlp_norm, w_mlp_in, w_mlp_out):
    batch, seq, _ = x.shape
    depth = w_in.shape[0]
    n = batch * seq
    o1 = SZ_FOX_QKV
    o2 = o1 + SZ_FORGET
    w_front = (w_in[:, :, :o1].astype(BF16), w_in[:, :, o2:].astype(BF16),
               jnp.pad(w_in[:, :, o1:o2], ((0, 0), (0, 0), (0, LANES - SZ_FORGET))).astype(BF16))
    wa, wb, wc, wo = (w.astype(BF16) for w in (w_up_fox, w_up_sb, w_up_dil, w_out))
    w1, w2 = w_mlp_in.astype(BF16), w_mlp_out.astype(BF16)
    pair = lambda g: jnp.tile(g.astype(F32), (1, LANES // HEAD_DIM))
    gains = jnp.stack([pair(q_norm_fox) * (SCALE * LOG2E), pair(k_norm_fox),
                       pair(q_norm_dil) * SCALE, pair(k_norm_dil)], axis=1)
    bf = jnp.pad(b_forget.astype(F32), ((0, 0), (0, LANES - SZ_FORGET)))[:, None, :]
    tables = _prep_tables()
    pos = positions.reshape(n // TM, 1, TM)

    x2 = x.reshape(n, D_MODEL)
    for l in range(depth):
        front = _front(x2, attn_norm[l][None, :], [w[l] for w in w_front], pos, gains[l], bf[l], tables, batch, seq)
        fqt, fk, fvt, sqt, sbk, svt = front[:6]
        dqs, dks, dvs, gates = front[6:9], front[9:12], front[12:15], front[15]
        out_a = _fox_attention(fqt, fk, fvt, batch, seq)
        out_b = _sb_attention(sqt, sbk, svt, batch, seq)
        dil = [_dil_attention(dqs[g], dks[g], dvs[g], g, batch, seq) for g in range(N_DIL_GROUPS)]
        x2 = _merge(x2, out_a, out_b, [d[0] for d in dil], [d[1] for d in dil], gates,
                    wa[l], wb[l], wc[l], wo[l], seq)
        x2 = _mlp(x2, mlp_norm[l][None, :], w1[l], w2[l])
    return x2.reshape(batch, seq, D_MODEL)
```
